```python
import math
import jax, jax.numpy as jnp
from jax import lax
import numpy as np

D_MODEL = 4096
BATCH = 2
SEQ = 8192
DEPTH = 2

NORM_EPS = 1e-6
N_MIXERS = 2
N_SSD_LAYERS = (DEPTH + 1) // 2
N_FOX_LAYERS = DEPTH // 2
SSD_EXPAND = 2
D_INNER = SSD_EXPAND * D_MODEL
SSD_HEAD_DIM = 64
SSD_HEADS = D_INNER // SSD_HEAD_DIM
SSD_GROUPS = 8
SSD_HEADS_PER_GROUP = SSD_HEADS // SSD_GROUPS
SSD_STATE = 128
SSD_CONV = 4
SSD_CHUNK = 128
SSD_CONV_DIM = D_INNER + 2 * SSD_GROUPS * SSD_STATE
SSD_PROJ_DIM = D_INNER + SSD_CONV_DIM + SSD_HEADS
FOX_HEAD_DIM = 128
FOX_HEADS = D_MODEL // FOX_HEAD_DIM
FOX_BLOCK = 128
FOX_PROJ_DIM = 3 * D_MODEL + FOX_HEADS
PEER_HEADS = 8
PEER_KEYS = 128
PEER_EXPERTS = PEER_KEYS * PEER_KEYS
PEER_TOPK = 16
PEER_QUERY_DIM = 256
PEER_HALF_DIM = PEER_QUERY_DIM // 2
PEER_BLOCK = 32
PLE_DIM = 256

kernel_name = 'hybrid_ssd_fox_peer_block'


def rmsnorm(x, w):
    xf = x.astype(jnp.float32)
    y = xf * lax.rsqrt(jnp.mean(xf * xf, axis=-1, keepdims=True) + NORM_EPS)
    return (y * w.astype(jnp.float32)).astype(x.dtype)


def gated_group_rmsnorm(y, z, w):
    g = y.astype(jnp.float32) * jax.nn.silu(z.astype(jnp.float32))
    shp = g.shape
    g = g.reshape(shp[:-1] + (SSD_GROUPS, D_INNER // SSD_GROUPS))
    g = g * lax.rsqrt(jnp.mean(g * g, axis=-1, keepdims=True) + NORM_EPS)
    return (g.reshape(shp) * w.astype(jnp.float32)).astype(y.dtype)


def causal_depthwise_conv(x, w):
    c = x.shape[-1]
    return lax.conv_general_dilated(
        x, w[:, None, :], window_strides=(1,), padding=[(SSD_CONV - 1, 0)],
        dimension_numbers=('NWC', 'WIO', 'NWC'), feature_group_count=c)


def ssd_chunked_scan(x, a, b, c):
    bsz, seq = x.shape[0], x.shape[1]
    nc = seq // SSD_CHUNK
    g, hg, pd, n, q = SSD_GROUPS, SSD_HEADS_PER_GROUP, SSD_HEAD_DIM, SSD_STATE, SSD_CHUNK
    xc = x.reshape(bsz, nc, q, g, hg, pd).transpose(1, 0, 2, 3, 4, 5)
    ac = a.reshape(bsz, nc, q, g, hg).transpose(1, 0, 2, 3, 4)
    bc = b.reshape(bsz, nc, q, g, n).transpose(1, 0, 2, 3, 4)
    cc = c.reshape(bsz, nc, q, g, n).transpose(1, 0, 2, 3, 4)
    causal = jnp.tril(jnp.ones((q, q), dtype=bool))[None, :, :, None, None]

    def step(state, inp):
        xq, aq, bq, cq = inp
        a_cs = jnp.cumsum(aq, axis=1)
        seg = a_cs[:, :, None] - a_cs[:, None, :]
        lmat = jnp.exp(jnp.where(causal, seg, -jnp.inf))
        cb = jnp.einsum('blgn,bsgn->blsg', cq, bq)
        y_diag = jnp.einsum('blsg,blsgh,bsghp->blghp', cb, lmat, xq)
        y_off = jnp.einsum('blgn,bghpn->blghp', cq, state) * jnp.exp(a_cs)[..., None]
        decay = jnp.exp(a_cs[:, -1:] - a_cs)
        new_state = state * jnp.exp(a_cs[:, -1])[..., None, None] + jnp.einsum(
            'bsgn,bsgh,bsghp->bghpn', bq, decay, xq)
        return new_state, y_diag + y_off

    state0 = jnp.zeros((bsz, g, hg, pd, n), jnp.float32)
    _, ys = lax.scan(step, state0, (xc, ac, bc, cc))
    return ys.transpose(1, 0, 2, 3, 4, 5).reshape(bsz, seq, SSD_HEADS, pd)


def ssd_mixer(hn, w_in, conv_w, conv_b, dt_bias, a_log, d_skip, norm_w, w_out):
    bsz, seq, _ = hn.shape
    proj = hn @ w_in
    z = proj[..., :D_INNER]
    xbc = proj[..., D_INNER:D_INNER + SSD_CONV_DIM]
    dt_raw = proj[..., D_INNER + SSD_CONV_DIM:]
    xbc = jax.nn.silu(causal_depthwise_conv(xbc, conv_w) + conv_b)
    gn = SSD_GROUPS * SSD_STATE
    xs = xbc[..., :D_INNER].reshape(bsz, seq, SSD_HEADS, SSD_HEAD_DIM).astype(jnp.float32)
    bm = xbc[..., D_INNER:D_INNER + gn].reshape(bsz, seq, SSD_GROUPS, SSD_STATE).astype(jnp.float32)
    cm = xbc[..., D_INNER + gn:].reshape(bsz, seq, SSD_GROUPS, SSD_STATE).astype(jnp.float32)
    dt = jax.nn.softplus(dt_raw.astype(jnp.float32) + dt_bias.astype(jnp.float32))
    a = -jnp.exp(a_log.astype(jnp.float32))
    y = ssd_chunked_scan(xs * dt[..., None], dt * a, bm, cm)
    y = y + d_skip.astype(jnp.float32)[:, None] * xs
    y = y.reshape(bsz, seq, D_INNER).astype(hn.dtype)
    y = gated_group_rmsnorm(y, z, norm_w)
    return y @ w_out


def fox_mixer(hn, w_in, b_f, w_out):
    bsz, seq, _ = hn.shape
    proj = hn @ w_in
    q = proj[..., :D_MODEL].reshape(bsz, seq, FOX_HEADS, FOX_HEAD_DIM)
    k = proj[..., D_MODEL:2 * D_MODEL].reshape(bsz, seq, FOX_HEADS, FOX_HEAD_DIM)
    v = proj[..., 2 * D_MODEL:3 * D_MODEL].reshape(bsz, seq, FOX_HEADS, FOX_HEAD_DIM)
    log_f = jax.nn.log_sigmoid(proj[..., 3 * D_MODEL:].astype(jnp.float32) + b_f.astype(jnp.float32))
    cum = jnp.cumsum(log_f, axis=1)
    nb = seq // FOX_BLOCK
    qb = q.reshape(bsz, nb, FOX_BLOCK, FOX_HEADS, FOX_HEAD_DIM).transpose(1, 0, 2, 3, 4)
    cqb = cum.reshape(bsz, nb, FOX_BLOCK, FOX_HEADS).transpose(1, 0, 2, 3)
    ck = cum.transpose(0, 2, 1)
    kpos = jnp.arange(seq)
    scale = FOX_HEAD_DIM ** -0.5

    def attend(args):
        q_i, c_i, blk = args
        s = jnp.einsum('bqhd,bkhd->bhqk', q_i, k).astype(jnp.float32) * scale
        s = s + c_i.transpose(0, 2, 1)[..., None] - ck[:, :, None, :]
        qpos = blk * FOX_BLOCK + jnp.arange(FOX_BLOCK)
        s = jnp.where(kpos[None, :] <= qpos[:, None], s, -jnp.inf)
        pr = jax.nn.softmax(s, axis=-1)
        return jnp.einsum('bhqk,bkhd->bqhd', pr.astype(v.dtype), v)

    o = lax.map(attend, (qb, cqb, jnp.arange(nb)))
    o = o.transpose(1, 0, 2, 3, 4).reshape(bsz, seq, D_MODEL)
    return o @ w_out


def peer_ffn(hn, w_q, subkeys, u, v):
    bsz, seq, _ = hn.shape
    q = (hn @ w_q).reshape(bsz, seq, PEER_HEADS, 2, PEER_HALF_DIM)
    s1 = jnp.einsum('blhd,kd->blhk', q[..., 0, :], subkeys[0]).astype(jnp.float32)
    s2 = jnp.einsum('blhd,kd->blhk', q[..., 1, :], subkeys[1]).astype(jnp.float32)
    v1, i1 = lax.top_k(s1, PEER_TOPK)
    v2, i2 = lax.top_k(s2, PEER_TOPK)
    cand = (v1[..., :, None] + v2[..., None, :]).reshape(bsz, seq, PEER_HEADS, PEER_TOPK * PEER_TOPK)
    sc, ci = lax.top_k(cand, PEER_TOPK)
    e_idx = (jnp.take_along_axis(i1, ci // PEER_TOPK, axis=-1) * PEER_KEYS
             + jnp.take_along_axis(i2, ci % PEER_TOPK, axis=-1))
    gate = jax.nn.softmax(sc, axis=-1)
    nb = seq // PEER_BLOCK

    def blocks(t):
        return t.reshape((bsz, nb, PEER_BLOCK) + t.shape[2:]).swapaxes(0, 1)

    def expert_block(args):
        xb, eb, gb = args
        act = jax.nn.gelu(jnp.einsum('bld,blhkd->blhk', xb, u[eb]), approximate=False)
        w = (gb * act.astype(jnp.float32)).astype(v.dtype)
        return jnp.einsum('blhk,blhkd->bld', w, v[eb]).astype(xb.dtype)

    y = lax.map(expert_block, (blocks(hn), blocks(e_idx), blocks(gate)))
    return y.swapaxes(0, 1).reshape(bsz, seq, D_MODEL)


def per_layer_embedding(h, p_i, ln_w, w_gate, w_proj):
    gate = jax.nn.sigmoid(rmsnorm(h, ln_w) @ w_gate)
    return (p_i @ w_proj) * gate


def setup_inputs(seed: int = 0) -> dict:
    key = jax.random.key(seed)
    ks = jax.random.split(key, 24)
    f32 = jnp.float32

    def nrm(k, shape, scale):
        return jax.random.normal(k, shape, f32) * scale

    def gain(k, shape):
        return 1.0 + 0.02 * jax.random.normal(k, shape, f32)

    dt = jnp.exp(jax.random.uniform(ks[6], (N_SSD_LAYERS, SSD_HEADS), f32,
                                    math.log(1e-3), math.log(1e-1)))
    dt_bias = dt + jnp.log(-jnp.expm1(-dt))
    return {
        'x': nrm(ks[0], (BATCH, SEQ, D_MODEL), 1.0),
        'p': nrm(ks[1], (DEPTH, BATCH, SEQ, PLE_DIM), 1.0),
        'ln_mix': gain(ks[2], (DEPTH, D_MODEL)),
        'ssd_w_in': nrm(ks[3], (N_SSD_LAYERS, D_MODEL, SSD_PROJ_DIM), D_MODEL ** -0.5),
        'ssd_conv_w': nrm(ks[4], (N_SSD_LAYERS, SSD_CONV, SSD_CONV_DIM), SSD_CONV ** -0.5),
        'ssd_conv_b': nrm(ks[5], (N_SSD_LAYERS, SSD_CONV_DIM), 0.02),
        'ssd_dt_bias': dt_bias,
        'ssd_a_log': jnp.log(jax.random.uniform(ks[7], (N_SSD_LAYERS, SSD_HEADS), f32, 1.0, 16.0)),
        'ssd_d_skip': 1.0 + 0.1 * jax.random.normal(ks[8], (N_SSD_LAYERS, SSD_HEADS), f32),
        'ssd_norm_w': gain(ks[9], (N_SSD_LAYERS, D_INNER)),
        'ssd_w_out': nrm(ks[10], (N_SSD_LAYERS, D_INNER, D_MODEL), D_INNER ** -0.5),
        'fox_w_in': nrm(ks[11], (N_FOX_LAYERS, D_MODEL, FOX_PROJ_DIM), D_MODEL ** -0.5),
        'fox_b_f': jax.random.uniform(ks[12], (N_FOX_LAYERS, FOX_HEADS), f32, 1.0, 4.0),
        'fox_w_out': nrm(ks[13], (N_FOX_LAYERS, D_MODEL, D_MODEL), D_MODEL ** -0.5),
        'ln_ffn': gain(ks[14], (DEPTH, D_MODEL)),
        'peer_w_q': nrm(ks[15], (DEPTH, D_MODEL, PEER_HEADS * PEER_QUERY_DIM), D_MODEL ** -0.5),
        'peer_subkeys': nrm(ks[16], (DEPTH, 2, PEER_KEYS, PEER_HALF_DIM), PEER_HALF_DIM ** -0.5),
        'peer_u': nrm(ks[17], (DEPTH, PEER_EXPERTS, D_MODEL), D_MODEL ** -0.5),
        'peer_v': nrm(ks[18], (DEPTH, PEER_EXPERTS, D_MODEL), (PEER_HEADS * PEER_TOPK) ** -0.5),
        'ln_ple': gain(ks[19], (DEPTH, D_MODEL)),
        'ple_w_gate': nrm(ks[20], (DEPTH, D_MODEL, D_MODEL), D_MODEL ** -0.5),
        'ple_w_proj': nrm(ks[21], (DEPTH, PLE_DIM, D_MODEL), PLE_DIM ** -0.5),
        'final_norm': gain(ks[22], (D_MODEL,)),
    }


def reference(x, p, ln_mix, ssd_w_in, ssd_conv_w, ssd_conv_b, ssd_dt_bias, ssd_a_log,
              ssd_d_skip, ssd_norm_w, ssd_w_out, fox_w_in, fox_b_f, fox_w_out, ln_ffn,
              peer_w_q, peer_subkeys, peer_u, peer_v, ln_ple, ple_w_gate, ple_w_proj,
              final_norm):
    h = x
    for i in range(DEPTH):
        hn = rmsnorm(h, ln_mix[i])
        j = i // N_MIXERS
        if i % N_MIXERS == 0:
            mix = ssd_mixer(hn, ssd_w_in[j], ssd_conv_w[j], ssd_conv_b[j], ssd_dt_bias[j],
                            ssd_a_log[j], ssd_d_skip[j], ssd_norm_w[j], ssd_w_out[j])
        else:
            mix = fox_mixer(hn, fox_w_in[j], fox_b_f[j], fox_w_out[j])
        h = h + mix
        h = h + peer_ffn(rmsnorm(h, ln_ffn[i]), peer_w_q[i], peer_subkeys[i], peer_u[i], peer_v[i])
        h = h + per_layer_embedding(h, p[i], ln_ple[i], ple_w_gate[i], ple_w_proj[i])
    return rmsnorm(h, final_norm)
```

```python
import functools
import math

import jax
import jax.numpy as jnp
from jax import lax
from jax.experimental import pallas as pl
from jax.experimental.pallas import tpu as pltpu

F32 = jnp.float32
BF16 = jnp.bfloat16
NEG_INF = float("-inf")

NORM_EPS = 1e-6
SSD_GROUPS = 8
SSD_HEAD_DIM = 64
SSD_CHUNK = 128
SSD_CONV = 4
FOX_HEAD_DIM = 128
PEER_HEADS = 8
PEER_TOPK = 16
PEER_KEYS_PER_STEP = 4

V7X_VMEM_BYTES = 64 * 1024 * 1024
VMEM_LIMIT = V7X_VMEM_BYTES - 8 * 1024 * 1024
LANES = 128
SUBLANES = 8


def _params(*sem):
    return pltpu.CompilerParams(dimension_semantics=sem, vmem_limit_bytes=VMEM_LIMIT)


def _pick(n, pref):
    t = min(n, pref)
    while n % t:
        t //= 2
    return t


def _rmsnorm_body(x_ref, w_ref, o_ref):
    x = x_ref[...].astype(F32)
    y = x * lax.rsqrt(jnp.mean(x * x, axis=-1, keepdims=True) + NORM_EPS)
    o_ref[...] = (y * w_ref[...]).astype(o_ref.dtype)


def rmsnorm(x, w, out_dtype):
    t, d = x.shape
    tm = _pick(t, 256)
    return pl.pallas_call(
        _rmsnorm_body,
        out_shape=jax.ShapeDtypeStruct((t, d), out_dtype),
        grid=(t // tm,),
        in_specs=[pl.BlockSpec((tm, d), lambda i: (i, 0)), pl.BlockSpec((1, d), lambda i: (0, 0))],
        out_specs=pl.BlockSpec((tm, d), lambda i: (i, 0)),
        compiler_params=_params("parallel"),
        name="rmsnorm",
    )(x, w.reshape(1, d).astype(F32))


def _add_rmsnorm_body(h_ref, y_ref, w_ref, h_out_ref, n_out_ref):
    h = h_ref[...] + y_ref[...]
    h_out_ref[...] = h
    y = h * lax.rsqrt(jnp.mean(h * h, axis=-1, keepdims=True) + NORM_EPS)
    n_out_ref[...] = (y * w_ref[...]).astype(n_out_ref.dtype)


def add_rmsnorm(h, y, w):
    t, d = h.shape
    tm = _pick(t, 256)
    row = pl.BlockSpec((tm, d), lambda i: (i, 0))
    return pl.pallas_call(
        _add_rmsnorm_body,
        out_shape=(jax.ShapeDtypeStruct((t, d), F32), jax.ShapeDtypeStruct((t, d), BF16)),
        grid=(t // tm,),
        in_specs=[row, row, pl.BlockSpec((1, d), lambda i: (0, 0))],
        out_specs=(row, row),
        compiler_params=_params("parallel"),
        name="add_rmsnorm",
    )(h, y, w.reshape(1, d).astype(F32))


def _matmul_body(*refs, nk, epilogue, n_extra):
    a_ref, b_ref = refs[0], refs[1]
    extra = refs[2:2 + n_extra]
    o_ref = refs[2 + n_extra]
    acc_ref = refs[3 + n_extra]
    k = pl.program_id(2)

    @pl.when(k == 0)
    def _():
        acc_ref[...] = jnp.zeros_like(acc_ref)

    acc_ref[...] += jnp.dot(a_ref[...], b_ref[...], preferred_element_type=F32)

    @pl.when(k == nk - 1)
    def _():
        o_ref[...] = epilogue(acc_ref[...], *extra).astype(o_ref.dtype)


def _epi_plain(acc):
    return acc


def _epi_residual(acc, res_ref):
    return res_ref[...] + acc


def _epi_ple(acc, h_ref, p_ref, wp_ref):
    proj = jnp.dot(p_ref[...], wp_ref[...], preferred_element_type=F32)
    return h_ref[...] + proj * jax.nn.sigmoid(acc)


def matmul(a, b, out_dtype, epilogue=_epi_plain, extra=(), extra_specs=None,
           tm=1024, tn=1024, tk=512, name="matmul"):
    m, kd = a.shape
    _, n = b.shape
    tm, tn, tk = _pick(m, tm), _pick(n, tn), _pick(kd, tk)
    nk = kd // tk
    if extra_specs is None:
        extra_specs = [pl.BlockSpec((tm, tn), lambda i, j, k: (i, j)) for _ in extra]
    else:
        extra_specs = extra_specs(tm, tn)
    return pl.pallas_call(
        functools.partial(_matmul_body, nk=nk, epilogue=epilogue, n_extra=len(extra)),
        out_shape=jax.ShapeDtypeStruct((m, n), out_dtype),
        grid=(m // tm, n // tn, nk),
        in_specs=[pl.BlockSpec((tm, tk), lambda i, j, k: (i, k)),
                  pl.BlockSpec((tk, tn), lambda i, j, k: (k, j))] + list(extra_specs),
        out_specs=pl.BlockSpec((tm, tn), lambda i, j, k: (i, j)),
        scratch_shapes=[pltpu.VMEM((tm, tn), F32)],
        compiler_params=_params("parallel", "parallel", "arbitrary"),
        name=name,
    )(a, b, *extra)


def _silu(x):
    return x * jax.nn.sigmoid(x)


def _ssd_body(z_ref, x_ref, b_ref, c_ref, dt_ref, dtt_ref, bias_ref, biast_ref, alog_ref, alogt_ref,
              wx_ref, wb_ref, wc_ref, bx_ref, bb_ref, bc_ref, dskip_ref, normw_ref, expand_ref,
              o_ref, state_ref, xpad_ref, bpad_ref, cpad_ref, *, q, hg, p):
    c_idx = pl.program_id(2)
    halo = SUBLANES

    @pl.when(c_idx == 0)
    def _():
        state_ref[...] = jnp.zeros_like(state_ref)
        xpad_ref[0:halo, :] = jnp.zeros((halo, xpad_ref.shape[1]), F32)
        bpad_ref[0:halo, :] = jnp.zeros((halo, bpad_ref.shape[1]), F32)
        cpad_ref[0:halo, :] = jnp.zeros((halo, cpad_ref.shape[1]), F32)

    def conv_silu(pad_ref, cur_ref, w_ref, bias_row_ref):
        pad_ref[halo:halo + q, :] = cur_ref[...].astype(F32)
        acc = bias_row_ref[...]
        for k in range(SSD_CONV):
            off = halo - (SSD_CONV - 1) + k
            acc = acc + w_ref[k:k + 1, :] * pad_ref[off:off + q, :]
        pad_ref[0:halo, :] = pad_ref[q:q + halo, :]
        return _silu(acc)

    xs = conv_silu(xpad_ref, x_ref, wx_ref, bx_ref)
    bm = conv_silu(bpad_ref, b_ref, wb_ref, bb_ref)
    cm = conv_silu(cpad_ref, c_ref, wc_ref, bc_ref)

    dt = jax.nn.softplus(dt_ref[...] + bias_ref[...])
    dtt = jax.nn.softplus(dtt_ref[...] + biast_ref[...])
    a = dt * (-jnp.exp(alog_ref[...]))
    at = dtt * (-jnp.exp(alogt_ref[...]))
    row = lax.broadcasted_iota(jnp.int32, (q, q), 0)
    col = lax.broadcasted_iota(jnp.int32, (q, q), 1)
    causal = row >= col
    tri = causal.astype(F32)
    hi = lax.Precision.HIGHEST
    a_cs = jnp.dot(tri, a, precision=hi, preferred_element_type=F32)
    a_cst = jnp.dot(at, (row <= col).astype(F32), precision=hi, preferred_element_type=F32)

    a_last = a_cs[q - 1:q, :]
    small = jnp.concatenate([dt, jnp.exp(a_cs), jnp.exp(a_last - a_cs)], axis=0)
    full = jnp.dot(small, expand_ref[...], precision=hi, preferred_element_type=F32)
    dt_full, dec_full, decay_full = full[0:q], full[q:2 * q], full[2 * q:3 * q]

    x_dt = xs * dt_full
    x_dt_b = x_dt.astype(BF16)
    bm_b = bm.astype(BF16)
    cm_b = cm.astype(BF16)

    cb = lax.dot_general(cm_b, bm_b, (((1,), (1,)), ((), ())), preferred_element_type=F32)

    state = state_ref[...]
    y = jnp.dot(cm_b, state.astype(BF16), preferred_element_type=F32) * dec_full

    lane = lax.broadcasted_iota(jnp.int32, (q, 2 * p), 1)
    y_parts = []
    for pair in range(hg // 2):
        ms = []
        for h in (2 * pair, 2 * pair + 1):
            seg = a_cs[:, h:h + 1] - a_cst[h:h + 1, :]
            ms.append((cb * jnp.exp(jnp.where(causal, seg, NEG_INF))).astype(BF16))
        lhs = jnp.concatenate(ms, axis=1)
        xp = x_dt_b[:, 2 * pair * p:(2 * pair + 2) * p]
        zero = jnp.zeros_like(xp)
        rhs = jnp.concatenate([jnp.where(lane < p, xp, zero), jnp.where(lane >= p, xp, zero)], axis=0)
        y_parts.append(jnp.dot(lhs, rhs, preferred_element_type=F32))
    y = y + jnp.concatenate(y_parts, axis=1)

    xd = (x_dt * decay_full).astype(BF16)
    upd = lax.dot_general(bm_b, xd, (((0,), (0,)), ((), ())), preferred_element_type=F32)
    state_ref[...] = state * dec_full[q - 1:q, :] + upd

    y = y + dskip_ref[...] * xs
    g = y * _silu(z_ref[...].astype(F32))
    g = g * lax.rsqrt(jnp.mean(g * g, axis=-1, keepdims=True) + NORM_EPS)
    o_ref[...] = (g * normw_ref[...]).astype(o_ref.dtype)


def ssd_core(zxbc, dt_raw, conv_w, conv_b, dt_bias, a_log, d_skip, norm_w, bsz, seq):
    t = bsz * seq
    heads = dt_bias.shape[0]
    g = SSD_GROUPS
    hg = heads // g
    p = SSD_HEAD_DIM
    gw = hg * p
    d_inner = heads * p
    n = (zxbc.shape[1] - 2 * d_inner) // (2 * g)
    q = SSD_CHUNK
    nc = seq // q
    assert gw % LANES == 0 and n % LANES == 0 and hg % 2 == 0 and 2 * p == LANES

    dtg = dt_raw.reshape(bsz, seq, g, hg).transpose(0, 2, 1, 3)
    dtgt = dtg.transpose(0, 1, 3, 2)
    bias = dt_bias.reshape(g, 1, hg).astype(F32)
    biast = dt_bias.reshape(g, hg, 1).astype(F32)
    alog = a_log.reshape(g, 1, hg).astype(F32)
    alogt = a_log.reshape(g, hg, 1).astype(F32)
    conv_w = conv_w.astype(F32)
    conv_b = conv_b.reshape(1, -1).astype(F32)
    dskip_full = jnp.repeat(d_skip.astype(F32), p).reshape(1, d_inner)
    normw = norm_w.reshape(1, d_inner).astype(F32)
    expand = (jnp.arange(gw)[None, :] // p == jnp.arange(hg)[:, None]).astype(F32)

    xblk, bblk, cblk = d_inner // gw, 2 * d_inner // n, (2 * d_inner + g * n) // n
    wx_blk, wb_blk, wc_blk = 0, d_inner // n, (d_inner + g * n) // n
    rows = lambda b, gi, c: b * nc + c
    in_specs = [
        pl.BlockSpec((q, gw), lambda b, gi, c: (rows(b, gi, c), gi)),
        pl.BlockSpec((q, gw), lambda b, gi, c: (rows(b, gi, c), xblk + gi)),
        pl.BlockSpec((q, n), lambda b, gi, c: (rows(b, gi, c), bblk + gi)),
        pl.BlockSpec((q, n), lambda b, gi, c: (rows(b, gi, c), cblk + gi)),
        pl.BlockSpec((None, None, q, hg), lambda b, gi, c: (b, gi, c, 0)),
        pl.BlockSpec((None, None, hg, q), lambda b, gi, c: (b, gi, 0, c)),
        pl.BlockSpec((None, 1, hg), lambda b, gi, c: (gi, 0, 0)),
        pl.BlockSpec((None, hg, 1), lambda b, gi, c: (gi, 0, 0)),
        pl.BlockSpec((None, 1, hg), lambda b, gi, c: (gi, 0, 0)),
        pl.BlockSpec((None, hg, 1), lambda b, gi, c: (gi, 0, 0)),
        pl.BlockSpec((SSD_CONV, gw), lambda b, gi, c: (0, wx_blk + gi)),
        pl.BlockSpec((SSD_CONV, n), lambda b, gi, c: (0, wb_blk + gi)),
        pl.BlockSpec((SSD_CONV, n), lambda b, gi, c: (0, wc_blk + gi)),
        pl.BlockSpec((1, gw), lambda b, gi, c: (0, wx_blk + gi)),
        pl.BlockSpec((1, n), lambda b, gi, c: (0, wb_blk + gi)),
        pl.BlockSpec((1, n), lambda b, gi, c: (0, wc_blk + gi)),
        pl.BlockSpec((1, gw), lambda b, gi, c: (0, gi)),
        pl.BlockSpec((1, gw), lambda b, gi, c: (0, gi)),
        pl.BlockSpec((hg, gw), lambda b, gi, c: (0, 0)),
    ]
    return pl.pallas_call(
        functools.partial(_ssd_body, q=q, hg=hg, p=p),
        out_shape=jax.ShapeDtypeStruct((t, d_inner), BF16),
        grid=(bsz, g, nc),
        in_specs=in_specs,
        out_specs=pl.BlockSpec((q, gw), lambda b, gi, c: (rows(b, gi, c), gi)),
        scratch_shapes=[pltpu.VMEM((n, gw), F32),
                        pltpu.VMEM((q + SUBLANES, gw), F32),
                        pltpu.VMEM((q + SUBLANES, n), F32),
                        pltpu.VMEM((q + SUBLANES, n), F32)],
        compiler_params=_params("parallel", "parallel", "arbitrary"),
        name="ssd_core",
    )(zxbc, zxbc, zxbc, zxbc, dtg, dtgt, bias, biast, alog, alogt,
      conv_w, conv_w, conv_w, conv_b, conv_b, conv_b, dskip_full, normw, expand)


def _fox_cum_body(f_ref, bf_ref, o_ref, carry_ref, *, tl):
    @pl.when(pl.program_id(1) == 0)
    def _():
        carry_ref[...] = jnp.zeros_like(carry_ref)

    logf = jax.nn.log_sigmoid(f_ref[...] + bf_ref[...])
    row = lax.broadcasted_iota(jnp.int32, (tl, tl), 0)
    col = lax.broadcasted_iota(jnp.int32, (tl, tl), 1)
    tri = (row >= col).astype(F32)
    cum = jnp.dot(tri, logf, precision=lax.Precision.HIGHEST, preferred_element_type=F32) + carry_ref[...]
    o_ref[...] = cum
    carry_ref[...] = cum[tl - 1:tl, :]


def fox_cum(f_raw, b_f, bsz, seq):
    t, w = f_raw.shape
    tl = _pick(seq, 256)
    nl = seq // tl
    return pl.pallas_call(
        functools.partial(_fox_cum_body, tl=tl),
        out_shape=jax.ShapeDtypeStruct((t, w), F32),
        grid=(bsz, nl),
        in_specs=[pl.BlockSpec((tl, w), lambda b, i: (b * nl + i, 0)),
                  pl.BlockSpec((1, w), lambda b, i: (0, 0))],
        out_specs=pl.BlockSpec((tl, w), lambda b, i: (b * nl + i, 0)),
        scratch_shapes=[pltpu.VMEM((1, w), F32)],
        compiler_params=_params("parallel", "arbitrary"),
        name="fox_cum",
    )(f_raw, b_f)


def _fox_attn_body(q_ref, k_ref, v_ref, cq_ref, ck_ref, o_ref, m_ref, l_ref, acc_ref, cqc_ref,
                   *, tq, tk, scale):
    h = pl.program_id(1)
    iq = pl.program_id(2)
    ik = pl.program_id(3)
    last_k = (iq * tq + tq - 1) // tk

    @pl.when(ik == 0)
    def _():
        m_ref[...] = jnp.full_like(m_ref, NEG_INF)
        l_ref[...] = jnp.zeros_like(l_ref)
        acc_ref[...] = jnp.zeros_like(acc_ref)
        lane = lax.broadcasted_iota(jnp.int32, cq_ref.shape, 1)
        cqc_ref[...] = jnp.sum(jnp.where(lane == h, cq_ref[...], 0.0), axis=1, keepdims=True)

    def step(masked):
        s = lax.dot_general(q_ref[...], k_ref[...], (((1,), (1,)), ((), ())), preferred_element_type=F32)
        s = s * scale + (cqc_ref[...] - ck_ref[...])
        if masked:
            qpos = iq * tq + lax.broadcasted_iota(jnp.int32, (tq, tk), 0)
            kpos = ik * tk + lax.broadcasted_iota(jnp.int32, (tq, tk), 1)
            s = jnp.where(kpos <= qpos, s, NEG_INF)
        m_prev = m_ref[...]
        m_new = jnp.maximum(m_prev, jnp.max(s, axis=1, keepdims=True))
        alpha = jnp.exp(m_prev - m_new)
        pr = jnp.exp(s - m_new)
        l_ref[...] = alpha * l_ref[...] + jnp.sum(pr, axis=1, keepdims=True)
        acc_ref[...] = alpha * acc_ref[...] + jnp.dot(pr.astype(BF16), v_ref[...], preferred_element_type=F32)
        m_ref[...] = m_new

    fully_visible = ik * tk + tk - 1 <= iq * tq

    @pl.when(fully_visible)
    def _():
        step(False)

    @pl.when(jnp.logical_and(jnp.logical_not(fully_visible), ik <= last_k))
    def _():
        step(True)

    @pl.when(ik == last_k)
    def _():
        o_ref[...] = (acc_ref[...] / l_ref[...]).astype(o_ref.dtype)


def fox_attention(qkv, cum, cum_t, bsz, seq, heads):
    t = bsz * seq
    hd = FOX_HEAD_DIM
    d = heads * hd
    tq = _pick(seq, 1024)
    tk = _pick(seq, 512)
    nq, nk = seq // tq, seq // tk
    scale = hd ** -0.5

    def kv_block(iq, ik):
        return jnp.minimum(ik, (iq * tq + tq - 1) // tk)

    return pl.pallas_call(
        functools.partial(_fox_attn_body, tq=tq, tk=tk, scale=scale),
        out_shape=jax.ShapeDtypeStruct((t, d), BF16),
        grid=(bsz, heads, nq, nk),
        in_specs=[
            pl.BlockSpec((tq, hd), lambda b, h, iq, ik: (b * nq + iq, h)),
            pl.BlockSpec((tk, hd), lambda b, h, iq, ik: (b * nk + kv_block(iq, ik), heads + h)),
            pl.BlockSpec((tk, hd), lambda b, h, iq, ik: (b * nk + kv_block(iq, ik), 2 * heads + h)),
            pl.BlockSpec((tq, cum.shape[1]), lambda b, h, iq, ik: (b * nq + iq, 0)),
            pl.BlockSpec((None, None, 1, tk), lambda b, h, iq, ik: (b, h, 0, kv_block(iq, ik))),
        ],
        out_specs=pl.BlockSpec((tq, hd), lambda b, h, iq, ik: (b * nq + iq, h)),
        scratch_shapes=[pltpu.VMEM((tq, 1), F32), pltpu.VMEM((tq, 1), F32),
                        pltpu.VMEM((tq, hd), F32), pltpu.VMEM((tq, 1), F32)],
        compiler_params=_params("parallel", "parallel", "parallel", "arbitrary"),
        name="fox_attention",
    )(qkv, qkv, qkv, cum, cum_t)


def _top_rows(blocks, k):
    out = []
    for _ in range(k):
        m = functools.reduce(jnp.maximum, blocks)
        m = jnp.max(m, axis=0, keepdims=True)
        out.append(m)
        blocks = [jnp.where(w == m, NEG_INF, w) for w in blocks]
    return out


def _peer_route_body(qt_ref, keys_ref, rows_ref, s2_ref, e2_ref, tau_ref, *, nkeys, kd):
    k = PEER_TOPK
    ni = PEER_KEYS_PER_STEP
    for h in range(PEER_HEADS):
        base = h * 2 * kd
        s1 = jnp.dot(keys_ref[0], qt_ref[base:base + kd, :], preferred_element_type=F32)
        s2 = jnp.dot(keys_ref[1], qt_ref[base + kd:base + 2 * kd, :], preferred_element_type=F32)
        rb = SUBLANES
        v1 = jnp.concatenate(_top_rows([s1[r:r + rb] for r in range(0, nkeys, rb)], k), axis=0)
        v2 = jnp.concatenate(_top_rows([s2[r:r + rb] for r in range(0, nkeys, rb)], k), axis=0)
        half = k // 2
        cands = [v1[0:1] + v2[0:half], v1[0:1] + v2[half:k]]
        cands += [v1[a:a + 1] + v2[0:half] for a in range(1, half)]
        cands += [v1[half:k] + v2[0:1]]
        sc = _top_rows(cands, k)
        m = sc[0]
        z = jnp.zeros_like(m)
        for r in sc:
            z = z + jnp.exp(r - m)
        e1 = jnp.exp(s1 - v1[0:1]) / z
        for blk in range(nkeys // ni):
            rows_ref[h, blk, 0:ni, :] = s1[blk * ni:(blk + 1) * ni]
            rows_ref[h, blk, ni:2 * ni, :] = e1[blk * ni:(blk + 1) * ni]
        s2_ref[h] = s2
        e2_ref[h] = jnp.exp(s2 - v2[0:1])
        tau_ref[h] = sc[k - 1]


def peer_route(qt, keys):
    rows, t = qt.shape
    _, nkeys, kd = keys.shape
    ni = PEER_KEYS_PER_STEP
    tm = _pick(t, 256)
    big = jax.ShapeDtypeStruct((PEER_HEADS, nkeys, t), F32)
    big_spec = pl.BlockSpec((PEER_HEADS, nkeys, tm), lambda i: (0, 0, i))
    return pl.pallas_call(
        functools.partial(_peer_route_body, nkeys=nkeys, kd=kd),
        out_shape=(jax.ShapeDtypeStruct((PEER_HEADS, nkeys // ni, 2 * ni, t), F32), big, big,
                   jax.ShapeDtypeStruct((PEER_HEADS, 1, t), F32)),
        grid=(t // tm,),
        in_specs=[pl.BlockSpec((rows, tm), lambda i: (0, i)),
                  pl.BlockSpec((2, nkeys, kd), lambda i: (0, 0, 0))],
        out_specs=(pl.BlockSpec((PEER_HEADS, nkeys // ni, 2 * ni, tm), lambda i: (0, 0, 0, i)),
                   big_spec, big_spec,
                   pl.BlockSpec((PEER_HEADS, 1, tm), lambda i: (0, 0, i))),
        compiler_params=_params("parallel"),
        name="peer_route",
    )(qt, keys)


def _gelu(x):
    return 0.5 * x * (1.0 + lax.erf(x * (1.0 / math.sqrt(2.0))))


def _peer_expert_body(xt_ref, u_ref, vt_ref, rows_ref, s2_ref, e2_ref, tau_ref, o_ref,
                      act_ref, w_ref, *, nkeys, tm):
    e = pl.program_id(1)
    ni = PEER_KEYS_PER_STEP
    act_ref[...] = jnp.dot(u_ref[...], xt_ref[...], preferred_element_type=F32)
    for ii in range(ni):
        for sub in range(tm // LANES):
            cols = slice(sub * LANES, (sub + 1) * LANES)
            gate = jnp.zeros((nkeys, LANES), F32)
            for h in range(PEER_HEADS):
                s = rows_ref[h, ii:ii + 1, cols] + s2_ref[h, :, cols]
                sel = jnp.where(s >= tau_ref[h, :, cols], e2_ref[h, :, cols], 0.0)
                gate = gate + sel * rows_ref[h, ni + ii:ni + ii + 1, cols]
            rows = slice(ii * nkeys, (ii + 1) * nkeys)
            w_ref[rows, cols] = (gate * _gelu(act_ref[rows, cols])).astype(BF16)
    upd = jnp.dot(vt_ref[...], w_ref[...], preferred_element_type=F32)

    @pl.when(e == 0)
    def _():
        o_ref[...] = upd

    @pl.when(e != 0)
    def _():
        o_ref[...] += upd


def peer_experts(xt, u, vt, rows, s2, e2, tau):
    d, t = xt.shape
    n_exp = u.shape[0]
    nkeys = s2.shape[1]
    ni = PEER_KEYS_PER_STEP
    tm = _pick(t, 512)
    eb = ni * nkeys
    once = dict(pipeline_mode=pl.Buffered(1))
    route_spec = lambda: pl.BlockSpec((PEER_HEADS, nkeys, tm), lambda i, e: (0, 0, i), **once)
    return pl.pallas_call(
        functools.partial(_peer_expert_body, nkeys=nkeys, tm=tm),
        out_shape=jax.ShapeDtypeStruct((d, t), F32),
        grid=(t // tm, n_exp // eb),
        in_specs=[pl.BlockSpec((d, tm), lambda i, e: (0, i), **once),
                  pl.BlockSpec((eb, d), lambda i, e: (e, 0)),
                  pl.BlockSpec((d, eb), lambda i, e: (0, e)),
                  pl.BlockSpec((PEER_HEADS, None, 2 * ni, tm), lambda i, e: (0, e, 0, i)),
                  route_spec(), route_spec(),
                  pl.BlockSpec((PEER_HEADS, 1, tm), lambda i, e: (0, 0, i), **once)],
        out_specs=pl.BlockSpec((d, tm), lambda i, e: (0, i)),
        scratch_shapes=[pltpu.VMEM((eb, tm), F32), pltpu.VMEM((eb, tm), BF16)],
        compiler_params=_params("parallel", "arbitrary"),
        name="peer_experts",
    )(xt, u, vt, rows, s2, e2, tau)


def _peer_ffn_t(hn, w_q, subkeys, u, v):
    xt = hn.T
    qt = matmul(w_q.T.astype(BF16), xt, BF16, name="peer_query")
    rows, s2, e2, tau = peer_route(qt, subkeys.astype(BF16))
    return peer_experts(xt, u.astype(BF16), v.T.astype(BF16), rows, s2, e2, tau)


def kernel(x, p, ln_mix, ssd_w_in, ssd_conv_w, ssd_conv_b, ssd_dt_bias, ssd_a_log, ssd_d_skip, ssd_norm_w, ssd_w_out, fox_w_in, fox_b_f, fox_w_out, ln_ffn, peer_w_q, peer_subkeys, peer_u, peer_v, ln_ple, ple_w_gate, ple_w_proj, final_norm):
    bsz, seq, d = x.shape
    depth = p.shape[0]
    t = bsz * seq
    h = x.reshape(t, d)
    for i in range(depth):
        hn = rmsnorm(h, ln_mix[i], BF16)
        j = i // 2
        if i % 2 == 0:
            d_inner = ssd_w_out.shape[1]
            conv_dim = ssd_conv_w.shape[2]
            w_in = ssd_w_in[j]
            zxbc = matmul(hn, w_in[:, :d_inner + conv_dim].astype(BF16), BF16, name="ssd_in_proj")
            dt_raw = matmul(hn, w_in[:, d_inner + conv_dim:].astype(BF16), F32, name="ssd_dt_proj")
            y = ssd_core(zxbc, dt_raw, ssd_conv_w[j], ssd_conv_b[j], ssd_dt_bias[j], ssd_a_log[j],
                         ssd_d_skip[j], ssd_norm_w[j], bsz, seq)
            h = matmul(y, ssd_w_out[j].astype(BF16), F32, epilogue=_epi_residual, extra=(h,),
                       name="ssd_out_proj")
        else:
            heads = fox_b_f.shape[1]
            w_in = fox_w_in[j]
            qkv = matmul(hn, w_in[:, :3 * d].astype(BF16), BF16, name="fox_in_proj")
            w_f = jnp.pad(w_in[:, 3 * d:], ((0, 0), (0, LANES - heads))).astype(BF16)
            f_raw = matmul(hn, w_f, F32, name="fox_gate_proj")
            b_f = jnp.pad(fox_b_f[j].astype(F32), (0, LANES - heads)).reshape(1, LANES)
            cum = fox_cum(f_raw, b_f, bsz, seq)
            cum_t = cum.reshape(bsz, seq, LANES)[:, :, :heads].transpose(0, 2, 1).reshape(bsz, heads, 1, seq)
            o = fox_attention(qkv, cum, cum_t, bsz, seq, heads)
            h = matmul(o, fox_w_out[j].astype(BF16), F32, epilogue=_epi_residual, extra=(h,),
                       name="fox_out_proj")
        hn = rmsnorm(h, ln_ffn[i], BF16)
        yt = _peer_ffn_t(hn, peer_w_q[i], peer_subkeys[i], peer_u[i], peer_v[i])
        h, hn = add_rmsnorm(h, yt.T, ln_ple[i])
        p_i = p[i].reshape(t, -1).astype(BF16)
        w_proj = ple_w_proj[i].astype(BF16)
        pdim = p_i.shape[1]
        h = matmul(hn, ple_w_gate[i].astype(BF16), F32, epilogue=_epi_ple, extra=(h, p_i, w_proj),
                   extra_specs=lambda tm, tn: [pl.BlockSpec((tm, tn), lambda a, b, c: (a, b)),
                                               pl.BlockSpec((tm, pdim), lambda a, b, c: (a, 0)),
                                               pl.BlockSpec((pdim, tn), lambda a, b, c: (0, b))],
                   name="ple")
    return rmsnorm(h, final_norm, F32).reshape(bsz, seq, d)
```

```python
import functools
import math

import jax
import jax.numpy as jnp
from jax import lax
from jax.experimental import pallas as pl
from jax.experimental.pallas import tpu as pltpu

F32 = jnp.float32
BF16 = jnp.bfloat16
NEG_INF = float("-inf")
LOG2E = math.log2(math.e)

NORM_EPS = 1e-6
SSD_GROUPS = 8
SSD_HEAD_DIM = 64
SSD_CHUNK = 128
SSD_CONV = 4
FOX_HEAD_DIM = 128
PEER_HEADS = 8
PEER_TOPK = 16
PEER_KEYS_PER_STEP = 4

V7X_VMEM_BYTES = 64 * 1024 * 1024
VMEM_LIMIT = V7X_VMEM_BYTES - 8 * 1024 * 1024
LANES = 128
SUBLANES = 8


def _params(*sem):
    return pltpu.CompilerParams(dimension_semantics=sem, vmem_limit_bytes=VMEM_LIMIT)


def _pick(n, pref):
    t = min(n, pref)
    while n % t:
        t //= 2
    return t


def _norm_rows(x, w_ref):
    return x * lax.rsqrt(jnp.mean(x * x, axis=-1, keepdims=True) + NORM_EPS) * w_ref[...]


def _rmsnorm_body(*refs, has_add, emit_n, emit_t, add_transposed):
    refs = list(refs)
    x = refs.pop(0)[...].astype(F32)
    if has_add:
        y = refs.pop(0)[...]
        x = x + (y.T if add_transposed else y)
    w_ref = refs.pop(0)
    if has_add:
        refs.pop(0)[...] = x
    n = _norm_rows(x, w_ref)
    if emit_n:
        o = refs.pop(0)
        o[...] = n.astype(o.dtype)
    if emit_t:
        o = refs.pop(0)
        o[...] = n.T.astype(o.dtype)


def rmsnorm(x, w, out_dtype=BF16, add=None, add_transposed=False, emit_n=True, emit_t=False):
    t, d = x.shape
    tm = _pick(t, 256)
    row = pl.BlockSpec((tm, d), lambda i: (i, 0))
    col = pl.BlockSpec((d, tm), lambda i: (0, i))
    in_specs, args = [row], [x]
    if add is not None:
        in_specs.append(col if add_transposed else row)
        args.append(add)
    in_specs.append(pl.BlockSpec((1, d), lambda i: (0, 0)))
    args.append(w.reshape(1, d).astype(F32))
    out_shape, out_specs = [], []
    if add is not None:
        out_shape.append(jax.ShapeDtypeStruct((t, d), F32))
        out_specs.append(row)
    if emit_n:
        out_shape.append(jax.ShapeDtypeStruct((t, d), out_dtype))
        out_specs.append(row)
    if emit_t:
        out_shape.append(jax.ShapeDtypeStruct((d, t), out_dtype))
        out_specs.append(col)
    return pl.pallas_call(
        functools.partial(_rmsnorm_body, has_add=add is not None, emit_n=emit_n, emit_t=emit_t,
                          add_transposed=add_transposed),
        out_shape=tuple(out_shape),
        grid=(t // tm,),
        in_specs=in_specs,
        out_specs=tuple(out_specs),
        compiler_params=_params("parallel"),
        name="rmsnorm",
    )(*args)


def _matmul_body(*refs, nk, epilogue, n_extra):
    a_ref, b_ref = refs[0], refs[1]
    extra = refs[2:2 + n_extra]
    o_ref = refs[2 + n_extra]
    if nk == 1:
        acc = jnp.dot(a_ref[...], b_ref[...], preferred_element_type=F32)
        o_ref[...] = epilogue(acc, *extra).astype(o_ref.dtype)
        return
    acc_ref = refs[3 + n_extra]
    k = pl.program_id(2)

    @pl.when(k == 0)
    def _():
        acc_ref[...] = jnp.dot(a_ref[...], b_ref[...], preferred_element_type=F32)

    @pl.when(jnp.logical_and(k > 0, k < nk - 1))
    def _():
        acc_ref[...] += jnp.dot(a_ref[...], b_ref[...], preferred_element_type=F32)

    @pl.when(k == nk - 1)
    def _():
        acc = acc_ref[...] + jnp.dot(a_ref[...], b_ref[...], preferred_element_type=F32)
        o_ref[...] = epilogue(acc, *extra).astype(o_ref.dtype)


def _epi_plain(acc):
    return acc


def _epi_residual(acc, res_ref):
    return res_ref[...] + acc


def _epi_ple(acc, h_ref, p_ref, wp_ref):
    proj = jnp.dot(p_ref[...], wp_ref[...], preferred_element_type=F32)
    return h_ref[...] + proj * jax.nn.sigmoid(acc)


def matmul(a, b, out_dtype, epilogue=_epi_plain, extra=(), extra_specs=None,
           tm=1024, tn=1024, tk=2048, name="matmul"):
    m, kd = a.shape
    _, n = b.shape
    tm, tn, tk = _pick(m, tm), _pick(n, tn), _pick(kd, tk)
    nk = kd // tk
    if extra_specs is None:
        extra_specs = [pl.BlockSpec((tm, tn), lambda i, j, k: (i, j)) for _ in extra]
    else:
        extra_specs = extra_specs(tm, tn)
    return pl.pallas_call(
        functools.partial(_matmul_body, nk=nk, epilogue=epilogue, n_extra=len(extra)),
        out_shape=jax.ShapeDtypeStruct((m, n), out_dtype),
        grid=(m // tm, n // tn, nk),
        in_specs=[pl.BlockSpec((tm, tk), lambda i, j, k: (i, k)),
                  pl.BlockSpec((tk, tn), lambda i, j, k: (k, j))] + list(extra_specs),
        out_specs=pl.BlockSpec((tm, tn), lambda i, j, k: (i, j)),
        scratch_shapes=[pltpu.VMEM((tm, tn), F32)] if nk > 1 else [],
        compiler_params=_params("parallel", "parallel", "arbitrary"),
        name=name,
    )(a, b, *extra)


def _silu(x):
    return x * jax.nn.sigmoid(x)


def _ssd_body(z_ref, x_ref, b_ref, c_ref, dt_ref, dtt_ref, bias_ref, biast_ref, alog_ref, alogt_ref,
              wx_ref, wb_ref, wc_ref, bx_ref, bb_ref, bc_ref, dskip_ref, normw_ref, expand_ref,
              o_ref, state_ref, xpad_ref, bpad_ref, cpad_ref, *, q, hg, p):
    c_idx = pl.program_id(2)
    halo = SUBLANES

    @pl.when(c_idx == 0)
    def _():
        state_ref[...] = jnp.zeros_like(state_ref)
        xpad_ref[0:halo, :] = jnp.zeros((halo, xpad_ref.shape[1]), F32)
        bpad_ref[0:halo, :] = jnp.zeros((halo, bpad_ref.shape[1]), F32)
        cpad_ref[0:halo, :] = jnp.zeros((halo, cpad_ref.shape[1]), F32)

    def conv_silu(pad_ref, cur_ref, w_ref, bias_row_ref):
        pad_ref[halo:halo + q, :] = cur_ref[...].astype(F32)
        acc = bias_row_ref[...]
        for k in range(SSD_CONV):
            off = halo - (SSD_CONV - 1) + k
            acc = acc + w_ref[k:k + 1, :] * pad_ref[off:off + q, :]
        pad_ref[0:halo, :] = pad_ref[q:q + halo, :]
        return _silu(acc)

    xs = conv_silu(xpad_ref, x_ref, wx_ref, bx_ref)
    bm = conv_silu(bpad_ref, b_ref, wb_ref, bb_ref)
    cm = conv_silu(cpad_ref, c_ref, wc_ref, bc_ref)

    dt = jax.nn.softplus(dt_ref[...] + bias_ref[...])
    dtt = jax.nn.softplus(dtt_ref[...] + biast_ref[...])
    a = dt * (-jnp.exp(alog_ref[...]))
    at = dtt * (-jnp.exp(alogt_ref[...]))
    row = lax.broadcasted_iota(jnp.int32, (q, q), 0)
    col = lax.broadcasted_iota(jnp.int32, (q, q), 1)
    causal = row >= col
    tri = causal.astype(F32)
    hi = lax.Precision.HIGHEST
    a_cs = jnp.dot(tri, a, precision=hi, preferred_element_type=F32)
    a_cst = jnp.dot(at, (row <= col).astype(F32), precision=hi, preferred_element_type=F32)

    a_last = a_cs[q - 1:q, :]
    small = jnp.concatenate([dt, jnp.exp(a_cs), jnp.exp(a_last - a_cs)], axis=0)
    full = jnp.dot(small, expand_ref[...], precision=hi, preferred_element_type=F32)
    dt_full, dec_full, decay_full = full[0:q], full[q:2 * q], full[2 * q:3 * q]

    x_dt = xs * dt_full
    x_dt_b = x_dt.astype(BF16)
    bm_b = bm.astype(BF16)
    cm_b = cm.astype(BF16)

    cb = lax.dot_general(cm_b, bm_b, (((1,), (1,)), ((), ())), preferred_element_type=F32)

    state = state_ref[...]
    y = jnp.dot(cm_b, state.astype(BF16), preferred_element_type=F32) * dec_full

    lane = lax.broadcasted_iota(jnp.int32, (q, 2 * p), 1)
    y_parts = []
    for pair in range(hg // 2):
        ms = []
        for h in (2 * pair, 2 * pair + 1):
            seg = a_cs[:, h:h + 1] - a_cst[h:h + 1, :]
            ms.append((cb * jnp.exp(jnp.where(causal, seg, NEG_INF))).astype(BF16))
        lhs = jnp.concatenate(ms, axis=1)
        xp = x_dt_b[:, 2 * pair * p:(2 * pair + 2) * p]
        zero = jnp.zeros_like(xp)
        rhs = jnp.concatenate([jnp.where(lane < p, xp, zero), jnp.where(lane >= p, xp, zero)], axis=0)
        y_parts.append(jnp.dot(lhs, rhs, preferred_element_type=F32))
    y = y + jnp.concatenate(y_parts, axis=1)

    xd = (x_dt * decay_full).astype(BF16)
    upd = lax.dot_general(bm_b, xd, (((0,), (0,)), ((), ())), preferred_element_type=F32)
    state_ref[...] = state * dec_full[q - 1:q, :] + upd

    y = y + dskip_ref[...] * xs
    g = y * _silu(z_ref[...].astype(F32))
    g = g * lax.rsqrt(jnp.mean(g * g, axis=-1, keepdims=True) + NORM_EPS)
    o_ref[...] = (g * normw_ref[...]).astype(o_ref.dtype)


def ssd_core(zxbc, dt_raw, conv_w, conv_b, dt_bias, a_log, d_skip, norm_w, bsz, seq):
    t = bsz * seq
    heads = dt_bias.shape[0]
    g = SSD_GROUPS
    hg = heads // g
    p = SSD_HEAD_DIM
    gw = hg * p
    d_inner = heads * p
    n = (zxbc.shape[1] - 2 * d_inner) // (2 * g)
    q = SSD_CHUNK
    nc = seq // q
    assert gw % LANES == 0 and n % LANES == 0 and hg % 2 == 0 and 2 * p == LANES

    dtg = dt_raw.reshape(bsz, seq, g, hg).transpose(0, 2, 1, 3)
    dtgt = dtg.transpose(0, 1, 3, 2)
    bias = dt_bias.reshape(g, 1, hg).astype(F32)
    biast = dt_bias.reshape(g, hg, 1).astype(F32)
    alog = a_log.reshape(g, 1, hg).astype(F32)
    alogt = a_log.reshape(g, hg, 1).astype(F32)
    conv_w = conv_w.astype(F32)
    conv_b = conv_b.reshape(1, -1).astype(F32)
    dskip_full = jnp.repeat(d_skip.astype(F32), p).reshape(1, d_inner)
    normw = norm_w.reshape(1, d_inner).astype(F32)
    expand = (jnp.arange(gw)[None, :] // p == jnp.arange(hg)[:, None]).astype(F32)

    xblk, bblk, cblk = d_inner // gw, 2 * d_inner // n, (2 * d_inner + g * n) // n
    wx_blk, wb_blk, wc_blk = 0, d_inner // n, (d_inner + g * n) // n
    rows = lambda b, gi, c: b * nc + c
    in_specs = [
        pl.BlockSpec((q, gw), lambda b, gi, c: (rows(b, gi, c), gi)),
        pl.BlockSpec((q, gw), lambda b, gi, c: (rows(b, gi, c), xblk + gi)),
        pl.BlockSpec((q, n), lambda b, gi, c: (rows(b, gi, c), bblk + gi)),
        pl.BlockSpec((q, n), lambda b, gi, c: (rows(b, gi, c), cblk + gi)),
        pl.BlockSpec((None, None, q, hg), lambda b, gi, c: (b, gi, c, 0)),
        pl.BlockSpec((None, None, hg, q), lambda b, gi, c: (b, gi, 0, c)),
        pl.BlockSpec((None, 1, hg), lambda b, gi, c: (gi, 0, 0)),
        pl.BlockSpec((None, hg, 1), lambda b, gi, c: (gi, 0, 0)),
        pl.BlockSpec((None, 1, hg), lambda b, gi, c: (gi, 0, 0)),
        pl.BlockSpec((None, hg, 1), lambda b, gi, c: (gi, 0, 0)),
        pl.BlockSpec((SSD_CONV, gw), lambda b, gi, c: (0, wx_blk + gi)),
        pl.BlockSpec((SSD_CONV, n), lambda b, gi, c: (0, wb_blk + gi)),
        pl.BlockSpec((SSD_CONV, n), lambda b, gi, c: (0, wc_blk + gi)),
        pl.BlockSpec((1, gw), lambda b, gi, c: (0, wx_blk + gi)),
        pl.BlockSpec((1, n), lambda b, gi, c: (0, wb_blk + gi)),
        pl.BlockSpec((1, n), lambda b, gi, c: (0, wc_blk + gi)),
        pl.BlockSpec((1, gw), lambda b, gi, c: (0, gi)),
        pl.BlockSpec((1, gw), lambda b, gi, c: (0, gi)),
        pl.BlockSpec((hg, gw), lambda b, gi, c: (0, 0)),
    ]
    return pl.pallas_call(
        functools.partial(_ssd_body, q=q, hg=hg, p=p),
        out_shape=jax.ShapeDtypeStruct((t, d_inner), BF16),
        grid=(bsz, g, nc),
        in_specs=in_specs,
        out_specs=pl.BlockSpec((q, gw), lambda b, gi, c: (rows(b, gi, c), gi)),
        scratch_shapes=[pltpu.VMEM((n, gw), F32),
                        pltpu.VMEM((q + SUBLANES, gw), F32),
                        pltpu.VMEM((q + SUBLANES, n), F32),
                        pltpu.VMEM((q + SUBLANES, n), F32)],
        compiler_params=_params("parallel", "parallel", "arbitrary"),
        name="ssd_core",
    )(zxbc, zxbc, zxbc, zxbc, dtg, dtgt, bias, biast, alog, alogt,
      conv_w, conv_w, conv_w, conv_b, conv_b, conv_b, dskip_full, normw, expand)


def _fox_cum_body(f_ref, bf_ref, o_ref, carry_ref, *, tl):
    @pl.when(pl.program_id(1) == 0)
    def _():
        carry_ref[...] = jnp.zeros_like(carry_ref)

    logf = jax.nn.log_sigmoid(f_ref[...] + bf_ref[...])
    row = lax.broadcasted_iota(jnp.int32, (tl, tl), 0)
    col = lax.broadcasted_iota(jnp.int32, (tl, tl), 1)
    tri = (row >= col).astype(F32)
    cum = jnp.dot(tri, logf, precision=lax.Precision.HIGHEST, preferred_element_type=F32) + carry_ref[...]
    o_ref[...] = cum * LOG2E
    carry_ref[...] = cum[tl - 1:tl, :]


def fox_cum(f_raw, b_f, bsz, seq):
    t, w = f_raw.shape
    tl = _pick(seq, 256)
    nl = seq // tl
    return pl.pallas_call(
        functools.partial(_fox_cum_body, tl=tl),
        out_shape=jax.ShapeDtypeStruct((t, w), F32),
        grid=(bsz, nl),
        in_specs=[pl.BlockSpec((tl, w), lambda b, i: (b * nl + i, 0)),
                  pl.BlockSpec((1, w), lambda b, i: (0, 0))],
        out_specs=pl.BlockSpec((tl, w), lambda b, i: (b * nl + i, 0)),
        scratch_shapes=[pltpu.VMEM((1, w), F32)],
        compiler_params=_params("parallel", "arbitrary"),
        name="fox_cum",
    )(f_raw, b_f)


def _fox_attn_body(iq_tab, ik_tab, q_ref, kt_ref, v_ref, ck_ref, o_ref, m_ref, l_ref, acc_ref, *, tb, rb):
    step = pl.program_id(2)
    iq = iq_tab[step]
    ik = ik_tab[step]

    @pl.when(ik == 0)
    def _():
        m_ref[...] = jnp.full_like(m_ref, NEG_INF)
        l_ref[...] = jnp.zeros_like(l_ref)
        acc_ref[...] = jnp.zeros_like(acc_ref)

    def run(masked):
        def logits(r0):
            return jnp.dot(q_ref[r0:r0 + rb, :], kt_ref[...], preferred_element_type=F32)

        starts = list(range(0, tb, rb))
        ahead = 2
        pending = [logits(r0) for r0 in starts[:ahead]]
        for idx, r0 in enumerate(starts):
            rows = slice(r0, r0 + rb)
            if idx + ahead < len(starts):
                pending.append(logits(starts[idx + ahead]))
            s = pending.pop(0) - ck_ref[...]
            if masked:
                qpos = r0 + lax.broadcasted_iota(jnp.int32, (rb, tb), 0)
                kpos = lax.broadcasted_iota(jnp.int32, (rb, tb), 1)
                s = jnp.where(kpos <= qpos, s, NEG_INF)
            m_prev = m_ref[rows, :]
            m_new = jnp.maximum(m_prev, jnp.max(s, axis=1, keepdims=True))
            alpha = jnp.exp2(m_prev - m_new)
            pr = jnp.exp2(s - jnp.tile(m_new, (1, tb // LANES)))
            l_ref[rows, :] = alpha * l_ref[rows, :] + jnp.sum(pr, axis=1, keepdims=True)
            acc_ref[rows, :] = alpha * acc_ref[rows, :] + jnp.dot(pr.astype(BF16), v_ref[...],
                                                                  preferred_element_type=F32)
            m_ref[rows, :] = m_new

    @pl.when(ik < iq)
    def _():
        run(False)

    @pl.when(ik == iq)
    def _():
        run(True)
        o_ref[...] = (acc_ref[...] / l_ref[...]).astype(o_ref.dtype)


def fox_attention(q, kt, v, ck_t, bsz, seq, heads):
    t, d = q.shape
    hd = FOX_HEAD_DIM
    assert hd == LANES
    tb = _pick(seq, 1024)
    nb = seq // tb
    pairs = [(iq, ik) for iq in range(nb) for ik in range(iq + 1)]
    iq_tab = jnp.asarray([pr[0] for pr in pairs], jnp.int32)
    ik_tab = jnp.asarray([pr[1] for pr in pairs], jnp.int32)
    grid_spec = pltpu.PrefetchScalarGridSpec(
        num_scalar_prefetch=2,
        grid=(bsz, heads, len(pairs)),
        in_specs=[
            pl.BlockSpec((tb, hd), lambda b, h, s, iqt, ikt: (b * nb + iqt[s], h)),
            pl.BlockSpec((hd, tb), lambda b, h, s, iqt, ikt: (h, b * nb + ikt[s])),
            pl.BlockSpec((tb, hd), lambda b, h, s, iqt, ikt: (b * nb + ikt[s], h)),
            pl.BlockSpec((None, None, 1, tb), lambda b, h, s, iqt, ikt: (b, h, 0, ikt[s])),
        ],
        out_specs=pl.BlockSpec((tb, hd), lambda b, h, s, iqt, ikt: (b * nb + iqt[s], h)),
        scratch_shapes=[pltpu.VMEM((tb, LANES), F32), pltpu.VMEM((tb, LANES), F32), pltpu.VMEM((tb, hd), F32)],
    )
    return pl.pallas_call(
        functools.partial(_fox_attn_body, tb=tb, rb=_pick(tb, 128)),
        out_shape=jax.ShapeDtypeStruct((t, d), BF16),
        grid_spec=grid_spec,
        compiler_params=_params("parallel", "parallel", "arbitrary"),
        name="fox_attention",
    )(iq_tab, ik_tab, q, kt, v, ck_t)


def _top_rows(blocks, k, with_rank=False):
    out = []
    ranks = [jnp.full(w.shape, float(k), F32) for w in blocks] if with_rank else None
    for r in range(k):
        m = functools.reduce(jnp.maximum, blocks)
        m = jnp.max(m, axis=0, keepdims=True)
        out.append(m)
        hit = [w == m for w in blocks]
        if with_rank:
            ranks = [jnp.where(e, float(r), rk) for e, rk in zip(hit, ranks)]
        blocks = [jnp.where(e, NEG_INF, w) for e, w in zip(hit, blocks)]
    return (out, ranks) if with_rank else out


def _bf16_pair_words(x):
    u = pltpu.bitcast(x, jnp.uint32)
    lsb = lax.shift_right_logical(u, jnp.uint32(16)) & jnp.uint32(1)
    r = lax.shift_right_logical(u + jnp.uint32(0x7FFF) + lsb, jnp.uint32(16))
    return r | lax.shift_left(r, jnp.uint32(16))


def _peer_route_body(qt_ref, keys_ref, rows_ref, r2_ref, e2_ref, *, nkeys, kd):
    k = PEER_TOPK
    ni = PEER_KEYS_PER_STEP
    rb = SUBLANES
    for h in range(PEER_HEADS):
        base = h * 2 * kd
        s1 = jnp.dot(keys_ref[0], qt_ref[base:base + kd, :], preferred_element_type=F32)
        s2 = jnp.dot(keys_ref[1], qt_ref[base + kd:base + 2 * kd, :], preferred_element_type=F32)
        v1 = jnp.concatenate(_top_rows([s1[r:r + rb] for r in range(0, nkeys, rb)], k), axis=0)
        v2_rows, rank2 = _top_rows([s2[r:r + rb] for r in range(0, nkeys, rb)], k, with_rank=True)
        v2 = jnp.concatenate(v2_rows, axis=0)
        half = k // 2
        cands = [v1[0:1] + v2[0:half], v1[0:1] + v2[half:k]]
        cands += [v1[a:a + 1] + v2[0:half] for a in range(1, half)]
        cands += [v1[half:k] + v2[0:1]]
        sc = _top_rows(cands, k)
        m, tau = sc[0], sc[k - 1]
        z = jnp.zeros_like(m)
        for r in sc:
            z = z + jnp.exp(r - m)
        e1 = jnp.exp(s1 - v1[0:1]) / z
        cnt = jnp.zeros_like(s1)
        for b in range(k):
            cnt = cnt + jnp.where(s1 + v2_rows[b] >= tau, 1.0, 0.0)
        cnt_w, e1_w = _bf16_pair_words(cnt), _bf16_pair_words(e1)
        for blk in range(nkeys // ni):
            rows_ref[h, blk, 0:ni, :] = cnt_w[blk * ni:(blk + 1) * ni]
            rows_ref[h, blk, ni:2 * ni, :] = e1_w[blk * ni:(blk + 1) * ni]
        half_keys = nkeys // 2
        r2 = jnp.concatenate(rank2, axis=0).astype(BF16)
        e2 = jnp.exp(s2 - v2[0:1]).astype(BF16)
        r2_ref[h * half_keys:(h + 1) * half_keys, :] = pltpu.bitcast(r2, jnp.uint32)
        e2_ref[h * half_keys:(h + 1) * half_keys, :] = pltpu.bitcast(e2, jnp.uint32)


def peer_route(qt, keys):
    rows, t = qt.shape
    _, nkeys, kd = keys.shape
    ni = PEER_KEYS_PER_STEP
    tm = _pick(t, 256)
    big = jax.ShapeDtypeStruct((PEER_HEADS * nkeys // 2, t), jnp.uint32)
    big_spec = pl.BlockSpec((PEER_HEADS * nkeys // 2, tm), lambda i: (0, i))
    return pl.pallas_call(
        functools.partial(_peer_route_body, nkeys=nkeys, kd=kd),
        out_shape=(jax.ShapeDtypeStruct((PEER_HEADS, nkeys // ni, 2 * ni, t), jnp.uint32), big, big),
        grid=(t // tm,),
        in_specs=[pl.BlockSpec((rows, tm), lambda i: (0, i)),
                  pl.BlockSpec((2, nkeys, kd), lambda i: (0, 0, 0))],
        out_specs=(pl.BlockSpec((PEER_HEADS, nkeys // ni, 2 * ni, tm), lambda i: (0, 0, 0, i)),
                   big_spec, big_spec),
        compiler_params=_params("parallel"),
        name="peer_route",
    )(qt, keys)


def _gelu(x):
    return 0.5 * x * (1.0 + lax.erf(x * (1.0 / math.sqrt(2.0))))


def _peer_expert_body(xt_ref, u_ref, vt_ref, rows_ref, r2_ref, e2_ref, o_ref, w_ref, *, nkeys, tm):
    e = pl.program_id(1)
    ni = PEER_KEYS_PER_STEP
    slot = e % 2

    @pl.when(e == 0)
    def _():
        o_ref[...] = jnp.zeros_like(o_ref)
        w_ref[1] = jnp.zeros(w_ref.shape[1:], BF16)

    act = jnp.dot(u_ref[...], xt_ref[...], preferred_element_type=F32)

    pack = 2 * SUBLANES
    half_keys = nkeys // 2

    def packed_rows(word_row):
        tile = pltpu.bitcast(jnp.broadcast_to(word_row, (SUBLANES, LANES)), BF16)
        return jnp.tile(tile, (nkeys // pack, 1))

    def gate_block(ii, sub, zero_row):
        rows = slice(ii * nkeys, (ii + 1) * nkeys)
        cols = slice(sub * LANES, (sub + 1) * LANES)
        gate = jnp.zeros((nkeys, LANES), BF16)
        for h in range(PEER_HEADS):
            cnt = packed_rows(rows_ref[h, ii:ii + 1, cols] + zero_row)
            e1 = packed_rows(rows_ref[h, ni + ii:ni + ii + 1, cols])
            keys = slice(h * half_keys, (h + 1) * half_keys)
            r2 = pltpu.bitcast(r2_ref[keys, cols], BF16)
            e2 = pltpu.bitcast(e2_ref[keys, cols], BF16)
            gate = gate + jnp.where(r2 < cnt, e2, jnp.zeros((), BF16)) * e1
        w_ref[slot, rows, cols] = (gate.astype(F32) * _gelu(act[rows, cols])).astype(BF16)

    blocks = [(ii, sub) for ii in range(ni) for sub in range(tm // LANES)]
    w_prev = w_ref[1 - slot]
    d = o_ref.shape[0]
    n_chunks = len(blocks) // 2
    dc = d // n_chunks
    for j in range(n_chunks):
        drows = slice(j * dc, (j + 1) * dc)
        upd = jnp.dot(vt_ref[drows, :], w_prev, preferred_element_type=F32)
        o_ref[drows, :] += upd
        bits = pltpu.bitcast(upd[0:SUBLANES, 0:LANES], jnp.uint32)
        zero_row = lax.shift_right_logical(lax.shift_right_logical(bits, jnp.uint32(16)), jnp.uint32(16))[0:1]
        gate_block(*blocks[2 * j], zero_row)
        gate_block(*blocks[2 * j + 1], zero_row)


def peer_experts(xt, u, vt, rows, r2, e2):
    d, t = xt.shape
    n_exp = u.shape[0]
    nkeys = 2 * r2.shape[0] // PEER_HEADS
    ni = PEER_KEYS_PER_STEP
    tm = _pick(t, 512)
    eb = ni * nkeys
    n_e = n_exp // eb
    once = dict(pipeline_mode=pl.Buffered(1))
    cur = lambda e: jnp.minimum(e, n_e - 1)
    prev = lambda e: jnp.maximum(e - 1, 0)
    route_spec = lambda: pl.BlockSpec((PEER_HEADS * nkeys // 2, tm), lambda i, e: (0, i), **once)
    return pl.pallas_call(
        functools.partial(_peer_expert_body, nkeys=nkeys, tm=tm),
        out_shape=jax.ShapeDtypeStruct((d, t), F32),
        grid=(t // tm, n_e + 1),
        in_specs=[pl.BlockSpec((d, tm), lambda i, e: (0, i), **once),
                  pl.BlockSpec((eb, d), lambda i, e: (cur(e), 0)),
                  pl.BlockSpec((d, eb), lambda i, e: (0, prev(e))),
                  pl.BlockSpec((PEER_HEADS, None, 2 * ni, tm), lambda i, e: (0, cur(e), 0, i)),
                  route_spec(), route_spec()],
        out_specs=pl.BlockSpec((d, tm), lambda i, e: (0, i)),
        scratch_shapes=[pltpu.VMEM((2, eb, tm), BF16)],
        compiler_params=_params("parallel", "arbitrary"),
        name="peer_experts",
    )(xt, u, vt, rows, r2, e2)


def _peer_ffn_t(xt, w_q, subkeys, u, v):
    qt = matmul(w_q.T.astype(BF16), xt, BF16, name="peer_query")
    rows, r2, e2 = peer_route(qt, subkeys.astype(BF16))
    return peer_experts(xt, u.astype(BF16), v.T.astype(BF16), rows, r2, e2)


def kernel(x, p, ln_mix, ssd_w_in, ssd_conv_w, ssd_conv_b, ssd_dt_bias, ssd_a_log, ssd_d_skip, ssd_norm_w, ssd_w_out, fox_w_in, fox_b_f, fox_w_out, ln_ffn, peer_w_q, peer_subkeys, peer_u, peer_v, ln_ple, ple_w_gate, ple_w_proj, final_norm):
    bsz, seq, d = x.shape
    depth = p.shape[0]
    t = bsz * seq
    h = x.reshape(t, d)
    for i in range(depth):
        j = i // 2
        if i % 2 == 0:
            (hn,) = rmsnorm(h, ln_mix[i])
            d_inner = ssd_w_out.shape[1]
            conv_dim = ssd_conv_w.shape[2]
            w_in = ssd_w_in[j]
            zxbc = matmul(hn, w_in[:, :d_inner + conv_dim].astype(BF16), BF16, name="ssd_in_proj")
            dt_raw = matmul(hn, w_in[:, d_inner + conv_dim:].astype(BF16), F32, name="ssd_dt_proj")
            y = ssd_core(zxbc, dt_raw, ssd_conv_w[j], ssd_conv_b[j], ssd_dt_bias[j], ssd_a_log[j],
                         ssd_d_skip[j], ssd_norm_w[j], bsz, seq)
            h = matmul(y, ssd_w_out[j].astype(BF16), F32, epilogue=_epi_residual, extra=(h,),
                       name="ssd_out_proj")
        else:
            hn, hnt = rmsnorm(h, ln_mix[i], emit_t=True)
            heads = fox_b_f.shape[1]
            w_in = fox_w_in[j]
            qscale = FOX_HEAD_DIM ** -0.5 * LOG2E
            q = matmul(hn, (w_in[:, :d] * qscale).astype(BF16), BF16, name="fox_q_proj")
            kt = matmul(w_in[:, d:2 * d].T.astype(BF16), hnt, BF16, name="fox_k_proj")
            v = matmul(hn, w_in[:, 2 * d:3 * d].astype(BF16), BF16, name="fox_v_proj")
            w_f = jnp.pad(w_in[:, 3 * d:], ((0, 0), (0, LANES - heads))).astype(BF16)
            f_raw = matmul(hn, w_f, F32, name="fox_gate_proj")
            b_f = jnp.pad(fox_b_f[j].astype(F32), (0, LANES - heads)).reshape(1, LANES)
            cum = fox_cum(f_raw, b_f, bsz, seq)
            ck_t = cum.reshape(bsz, seq, LANES)[:, :, :heads].transpose(0, 2, 1).reshape(bsz, heads, 1, seq)
            o = fox_attention(q, kt, v, ck_t, bsz, seq, heads)
            h = matmul(o, fox_w_out[j].astype(BF16), F32, epilogue=_epi_residual, extra=(h,),
                       name="fox_out_proj")
        (hnt,) = rmsnorm(h, ln_ffn[i], emit_n=False, emit_t=True)
        yt = _peer_ffn_t(hnt, peer_w_q[i], peer_subkeys[i], peer_u[i], peer_v[i])
        h, hn = rmsnorm(h, ln_ple[i], add=yt, add_transposed=True)
        p_i = p[i].reshape(t, -1).astype(BF16)
        w_proj = ple_w_proj[i].astype(BF16)
        pdim = p_i.shape[1]
        h = matmul(hn, ple_w_gate[i].astype(BF16), F32, epilogue=_epi_ple, extra=(h, p_i, w_proj),
                   extra_specs=lambda tm, tn: [pl.BlockSpec((tm, tn), lambda a, b, c: (a, b)),
                                               pl.BlockSpec((tm, pdim), lambda a, b, c: (a, 0)),
                                               pl.BlockSpec((pdim, tn), lambda a, b, c: (0, b))],
                   name="ple")
    (out,) = rmsnorm(h, final_norm, out_dtype=F32)
    return out.reshape(bsz, seq, d)
```

```python
import functools
import math

import jax
import jax.numpy as jnp
from jax import lax
from jax.experimental import pallas as pl
from jax.experimental.pallas import tpu as pltpu

F32 = jnp.float32
BF16 = jnp.bfloat16
NEG_INF = float("-inf")
LOG2E = math.log2(math.e)

NORM_EPS = 1e-6
SSD_GROUPS = 8
SSD_HEAD_DIM = 64
SSD_CHUNK = 128
SSD_CONV = 4
FOX_HEAD_DIM = 128
PEER_HEADS = 8
PEER_TOPK = 16
PEER_KEYS_PER_STEP = 4

V7X_VMEM_BYTES = 64 * 1024 * 1024
VMEM_LIMIT = V7X_VMEM_BYTES - 8 * 1024 * 1024
LANES = 128
SUBLANES = 8


def _params(*sem):
    return pltpu.CompilerParams(dimension_semantics=sem, vmem_limit_bytes=VMEM_LIMIT)


def _pick(n, pref):
    t = min(n, pref)
    while n % t:
        t //= 2
    return t


def _norm_rows(x, w_ref):
    return x * lax.rsqrt(jnp.mean(x * x, axis=-1, keepdims=True) + NORM_EPS) * w_ref[...]


def _rmsnorm_body(*refs, has_add, emit_n, emit_t, add_transposed):
    refs = list(refs)
    x = refs.pop(0)[...].astype(F32)
    if has_add:
        y = refs.pop(0)[...]
        x = x + (y.T if add_transposed else y)
    w_ref = refs.pop(0)
    if has_add:
        refs.pop(0)[...] = x
    n = _norm_rows(x, w_ref)
    if emit_n:
        o = refs.pop(0)
        o[...] = n.astype(o.dtype)
    if emit_t:
        o = refs.pop(0)
        o[...] = n.T.astype(o.dtype)


def rmsnorm(x, w, out_dtype=BF16, add=None, add_transposed=False, emit_n=True, emit_t=False):
    t, d = x.shape
    tm = _pick(t, 256)
    row = pl.BlockSpec((tm, d), lambda i: (i, 0))
    col = pl.BlockSpec((d, tm), lambda i: (0, i))
    in_specs, args = [row], [x]
    if add is not None:
        in_specs.append(col if add_transposed else row)
        args.append(add)
    in_specs.append(pl.BlockSpec((1, d), lambda i: (0, 0)))
    args.append(w.reshape(1, d).astype(F32))
    out_shape, out_specs = [], []
    if add is not None:
        out_shape.append(jax.ShapeDtypeStruct((t, d), F32))
        out_specs.append(row)
    if emit_n:
        out_shape.append(jax.ShapeDtypeStruct((t, d), out_dtype))
        out_specs.append(row)
    if emit_t:
        out_shape.append(jax.ShapeDtypeStruct((d, t), out_dtype))
        out_specs.append(col)
    return pl.pallas_call(
        functools.partial(_rmsnorm_body, has_add=add is not None, emit_n=emit_n, emit_t=emit_t,
                          add_transposed=add_transposed),
        out_shape=tuple(out_shape),
        grid=(t // tm,),
        in_specs=in_specs,
        out_specs=tuple(out_specs),
        compiler_params=_params("parallel"),
        name="rmsnorm",
    )(*args)


def _matmul_body(*refs, nk, epilogue, n_extra):
    a_ref, b_ref = refs[0], refs[1]
    extra = refs[2:2 + n_extra]
    o_ref = refs[2 + n_extra]
    if nk == 1:
        acc = jnp.dot(a_ref[...], b_ref[...], preferred_element_type=F32)
        o_ref[...] = epilogue(acc, *extra).astype(o_ref.dtype)
        return
    acc_ref = refs[3 + n_extra]
    k = pl.program_id(2)

    @pl.when(k == 0)
    def _():
        acc_ref[...] = jnp.dot(a_ref[...], b_ref[...], preferred_element_type=F32)

    @pl.when(jnp.logical_and(k > 0, k < nk - 1))
    def _():
        acc_ref[...] += jnp.dot(a_ref[...], b_ref[...], preferred_element_type=F32)

    @pl.when(k == nk - 1)
    def _():
        acc = acc_ref[...] + jnp.dot(a_ref[...], b_ref[...], preferred_element_type=F32)
        o_ref[...] = epilogue(acc, *extra).astype(o_ref.dtype)


def _epi_plain(acc):
    return acc


def _epi_residual(acc, res_ref):
    return res_ref[...] + acc


def _epi_ple(acc, h_ref, p_ref, wp_ref):
    proj = jnp.dot(p_ref[...], wp_ref[...], preferred_element_type=F32)
    return h_ref[...] + proj * jax.nn.sigmoid(acc)


def matmul(a, b, out_dtype, epilogue=_epi_plain, extra=(), extra_specs=None,
           tm=1024, tn=1024, tk=2048, name="matmul"):
    m, kd = a.shape
    _, n = b.shape
    tm, tn, tk = _pick(m, tm), _pick(n, tn), _pick(kd, tk)
    nk = kd // tk
    if extra_specs is None:
        extra_specs = [pl.BlockSpec((tm, tn), lambda i, j, k: (i, j)) for _ in extra]
    else:
        extra_specs = extra_specs(tm, tn)
    return pl.pallas_call(
        functools.partial(_matmul_body, nk=nk, epilogue=epilogue, n_extra=len(extra)),
        out_shape=jax.ShapeDtypeStruct((m, n), out_dtype),
        grid=(m // tm, n // tn, nk),
        in_specs=[pl.BlockSpec((tm, tk), lambda i, j, k: (i, k)),
                  pl.BlockSpec((tk, tn), lambda i, j, k: (k, j))] + list(extra_specs),
        out_specs=pl.BlockSpec((tm, tn), lambda i, j, k: (i, j)),
        scratch_shapes=[pltpu.VMEM((tm, tn), F32)] if nk > 1 else [],
        compiler_params=_params("parallel", "parallel", "arbitrary"),
        name=name,
    )(a, b, *extra)


def _silu(x):
    return x * jax.nn.sigmoid(x)


def _ssd_body(z_ref, x_ref, b_ref, c_ref, dt_ref, dtt_ref, bias_ref, biast_ref, alog_ref, alogt_ref,
              wx_ref, wb_ref, wc_ref, bx_ref, bb_ref, bc_ref, dskip_ref, normw_ref, expand_ref,
              o_ref, state_ref, xpad_ref, bpad_ref, cpad_ref, *, q, hg, p):
    c_idx = pl.program_id(2)
    halo = SUBLANES

    @pl.when(c_idx == 0)
    def _():
        state_ref[...] = jnp.zeros_like(state_ref)
        xpad_ref[0:halo, :] = jnp.zeros((halo, xpad_ref.shape[1]), F32)
        bpad_ref[0:halo, :] = jnp.zeros((halo, bpad_ref.shape[1]), F32)
        cpad_ref[0:halo, :] = jnp.zeros((halo, cpad_ref.shape[1]), F32)

    def conv_silu(pad_ref, cur_ref, w_ref, bias_row_ref):
        pad_ref[halo:halo + q, :] = cur_ref[...].astype(F32)
        acc = bias_row_ref[...]
        for k in range(SSD_CONV):
            off = halo - (SSD_CONV - 1) + k
            acc = acc + w_ref[k:k + 1, :] * pad_ref[off:off + q, :]
        pad_ref[0:halo, :] = pad_ref[q:q + halo, :]
        return _silu(acc)

    xs = conv_silu(xpad_ref, x_ref, wx_ref, bx_ref)
    bm = conv_silu(bpad_ref, b_ref, wb_ref, bb_ref)
    cm = conv_silu(cpad_ref, c_ref, wc_ref, bc_ref)

    dt = jax.nn.softplus(dt_ref[...] + bias_ref[...])
    dtt = jax.nn.softplus(dtt_ref[...] + biast_ref[...])
    a = dt * (-jnp.exp(alog_ref[...]))
    at = dtt * (-jnp.exp(alogt_ref[...]))
    row = lax.broadcasted_iota(jnp.int32, (q, q), 0)
    col = lax.broadcasted_iota(jnp.int32, (q, q), 1)
    causal = row >= col
    tri = causal.astype(F32)
    hi = lax.Precision.HIGHEST
    a_cs = jnp.dot(tri, a, precision=hi, preferred_element_type=F32)
    a_cst = jnp.dot(at, (row <= col).astype(F32), precision=hi, preferred_element_type=F32)

    a_last = a_cs[q - 1:q, :]
    small = jnp.concatenate([dt, jnp.exp(a_cs), jnp.exp(a_last - a_cs)], axis=0)
    small_hi = small.astype(BF16)
    small_lo = (small - small_hi.astype(F32)).astype(BF16)
    full = (jnp.dot(small_hi, expand_ref[...], preferred_element_type=F32)
            + jnp.dot(small_lo, expand_ref[...], preferred_element_type=F32))
    dt_full, dec_full, decay_full = full[0:q], full[q:2 * q], full[2 * q:3 * q]

    x_dt = xs * dt_full
    x_dt_b = x_dt.astype(BF16)
    bm_b = bm.astype(BF16)
    cm_b = cm.astype(BF16)

    cb = lax.dot_general(cm_b, bm_b, (((1,), (1,)), ((), ())), preferred_element_type=F32)

    state = state_ref[...]
    y = jnp.dot(cm_b, state.astype(BF16), preferred_element_type=F32) * dec_full

    lane = lax.broadcasted_iota(jnp.int32, (q, 2 * p), 1)
    y_parts = []
    for pair in range(hg // 2):
        ms = []
        for h in (2 * pair, 2 * pair + 1):
            seg = a_cs[:, h:h + 1] - a_cst[h:h + 1, :]
            ms.append((cb * jnp.exp(jnp.where(causal, seg, NEG_INF))).astype(BF16))
        lhs = jnp.concatenate(ms, axis=1)
        xp = x_dt_b[:, 2 * pair * p:(2 * pair + 2) * p]
        zero = jnp.zeros_like(xp)
        rhs = jnp.concatenate([jnp.where(lane < p, xp, zero), jnp.where(lane >= p, xp, zero)], axis=0)
        y_parts.append(jnp.dot(lhs, rhs, preferred_element_type=F32))
    y = y + jnp.concatenate(y_parts, axis=1)

    xd = (x_dt * decay_full).astype(BF16)
    upd = lax.dot_general(bm_b, xd, (((0,), (0,)), ((), ())), preferred_element_type=F32)
    state_ref[...] = state * dec_full[q - 1:q, :] + upd

    y = y + dskip_ref[...] * xs
    g = y * _silu(z_ref[...].astype(F32))
    g = g * lax.rsqrt(jnp.mean(g * g, axis=-1, keepdims=True) + NORM_EPS)
    o_ref[...] = (g * normw_ref[...]).astype(o_ref.dtype)


def ssd_core(zxbc, dt_raw, conv_w, conv_b, dt_bias, a_log, d_skip, norm_w, bsz, seq):
    t = bsz * seq
    heads = dt_bias.shape[0]
    g = SSD_GROUPS
    hg = heads // g
    p = SSD_HEAD_DIM
    gw = hg * p
    d_inner = heads * p
    n = (zxbc.shape[1] - 2 * d_inner) // (2 * g)
    q = SSD_CHUNK
    nc = seq // q
    assert gw % LANES == 0 and n % LANES == 0 and hg % 2 == 0 and 2 * p == LANES

    dtg = dt_raw.reshape(bsz, seq, g, hg).transpose(0, 2, 1, 3)
    dtgt = dtg.transpose(0, 1, 3, 2)
    bias = dt_bias.reshape(g, 1, hg).astype(F32)
    biast = dt_bias.reshape(g, hg, 1).astype(F32)
    alog = a_log.reshape(g, 1, hg).astype(F32)
    alogt = a_log.reshape(g, hg, 1).astype(F32)
    conv_w = conv_w.astype(F32)
    conv_b = conv_b.reshape(1, -1).astype(F32)
    dskip_full = jnp.repeat(d_skip.astype(F32), p).reshape(1, d_inner)
    normw = norm_w.reshape(1, d_inner).astype(F32)
    expand = (jnp.arange(gw)[None, :] // p == jnp.arange(hg)[:, None]).astype(BF16)

    xblk, bblk, cblk = d_inner // gw, 2 * d_inner // n, (2 * d_inner + g * n) // n
    wx_blk, wb_blk, wc_blk = 0, d_inner // n, (d_inner + g * n) // n
    rows = lambda b, gi, c: b * nc + c
    in_specs = [
        pl.BlockSpec((q, gw), lambda b, gi, c: (rows(b, gi, c), gi)),
        pl.BlockSpec((q, gw), lambda b, gi, c: (rows(b, gi, c), xblk + gi)),
        pl.BlockSpec((q, n), lambda b, gi, c: (rows(b, gi, c), bblk + gi)),
        pl.BlockSpec((q, n), lambda b, gi, c: (rows(b, gi, c), cblk + gi)),
        pl.BlockSpec((None, None, q, hg), lambda b, gi, c: (b, gi, c, 0)),
        pl.BlockSpec((None, None, hg, q), lambda b, gi, c: (b, gi, 0, c)),
        pl.BlockSpec((None, 1, hg), lambda b, gi, c: (gi, 0, 0)),
        pl.BlockSpec((None, hg, 1), lambda b, gi, c: (gi, 0, 0)),
        pl.BlockSpec((None, 1, hg), lambda b, gi, c: (gi, 0, 0)),
        pl.BlockSpec((None, hg, 1), lambda b, gi, c: (gi, 0, 0)),
        pl.BlockSpec((SSD_CONV, gw), lambda b, gi, c: (0, wx_blk + gi)),
        pl.BlockSpec((SSD_CONV, n), lambda b, gi, c: (0, wb_blk + gi)),
        pl.BlockSpec((SSD_CONV, n), lambda b, gi, c: (0, wc_blk + gi)),
        pl.BlockSpec((1, gw), lambda b, gi, c: (0, wx_blk + gi)),
        pl.BlockSpec((1, n), lambda b, gi, c: (0, wb_blk + gi)),
        pl.BlockSpec((1, n), lambda b, gi, c: (0, wc_blk + gi)),
        pl.BlockSpec((1, gw), lambda b, gi, c: (0, gi)),
        pl.BlockSpec((1, gw), lambda b, gi, c: (0, gi)),
        pl.BlockSpec((hg, gw), lambda b, gi, c: (0, 0)),
    ]
    return pl.pallas_call(
        functools.partial(_ssd_body, q=q, hg=hg, p=p),
        out_shape=jax.ShapeDtypeStruct((t, d_inner), BF16),
        grid=(bsz, g, nc),
        in_specs=in_specs,
        out_specs=pl.BlockSpec((q, gw), lambda b, gi, c: (rows(b, gi, c), gi)),
        scratch_shapes=[pltpu.VMEM((n, gw), F32),
                        pltpu.VMEM((q + SUBLANES, gw), F32),
                        pltpu.VMEM((q + SUBLANES, n), F32),
                        pltpu.VMEM((q + SUBLANES, n), F32)],
        compiler_params=_params("parallel", "parallel", "arbitrary"),
        name="ssd_core",
    )(zxbc, zxbc, zxbc, zxbc, dtg, dtgt, bias, biast, alog, alogt,
      conv_w, conv_w, conv_w, conv_b, conv_b, conv_b, dskip_full, normw, expand)


def _fox_cum_body(f_ref, bf_ref, o_ref, carry_ref, *, tl):
    @pl.when(pl.program_id(1) == 0)
    def _():
        carry_ref[...] = jnp.zeros_like(carry_ref)

    logf = jax.nn.log_sigmoid(f_ref[...] + bf_ref[...])
    row = lax.broadcasted_iota(jnp.int32, (tl, tl), 0)
    col = lax.broadcasted_iota(jnp.int32, (tl, tl), 1)
    tri = (row >= col).astype(F32)
    cum = jnp.dot(tri, logf, precision=lax.Precision.HIGHEST, preferred_element_type=F32) + carry_ref[...]
    o_ref[...] = cum * LOG2E
    carry_ref[...] = cum[tl - 1:tl, :]


def fox_cum(f_raw, b_f, bsz, seq):
    t, w = f_raw.shape
    tl = _pick(seq, 256)
    nl = seq // tl
    return pl.pallas_call(
        functools.partial(_fox_cum_body, tl=tl),
        out_shape=jax.ShapeDtypeStruct((t, w), F32),
        grid=(bsz, nl),
        in_specs=[pl.BlockSpec((tl, w), lambda b, i: (b * nl + i, 0)),
                  pl.BlockSpec((1, w), lambda b, i: (0, 0))],
        out_specs=pl.BlockSpec((tl, w), lambda b, i: (b * nl + i, 0)),
        scratch_shapes=[pltpu.VMEM((1, w), F32)],
        compiler_params=_params("parallel", "arbitrary"),
        name="fox_cum",
    )(f_raw, b_f)


def _fox_attn_body(iq_tab, ik_tab, q_ref, kt_ref, v_ref, ck_ref, o_ref, m_ref, l_ref, acc_ref, *, tb, rb):
    step = pl.program_id(2)
    iq = iq_tab[step]
    ik = ik_tab[step]

    @pl.when(ik == 0)
    def _():
        m_ref[...] = jnp.full_like(m_ref, NEG_INF)
        l_ref[...] = jnp.zeros_like(l_ref)
        acc_ref[...] = jnp.zeros_like(acc_ref)

    def run(masked):
        def logits(r0):
            return jnp.dot(q_ref[r0:r0 + rb, :], kt_ref[...], preferred_element_type=F32)

        starts = list(range(0, tb, rb))
        ahead = 2
        pending = [logits(r0) for r0 in starts[:ahead]]
        for idx, r0 in enumerate(starts):
            rows = slice(r0, r0 + rb)
            if idx + ahead < len(starts):
                pending.append(logits(starts[idx + ahead]))
            s = pending.pop(0) - ck_ref[...]
            if masked:
                qpos = r0 + lax.broadcasted_iota(jnp.int32, (rb, tb), 0)
                kpos = lax.broadcasted_iota(jnp.int32, (rb, tb), 1)
                s = jnp.where(kpos <= qpos, s, NEG_INF)
            m_prev = m_ref[rows, :]
            m_new = jnp.maximum(m_prev, jnp.max(s, axis=1, keepdims=True))
            alpha = jnp.exp2(m_prev - m_new)
            pr = jnp.exp2(s - jnp.tile(m_new, (1, tb // LANES)))
            l_ref[rows, :] = alpha * l_ref[rows, :] + jnp.sum(pr, axis=1, keepdims=True)
            acc_ref[rows, :] = alpha * acc_ref[rows, :] + jnp.dot(pr.astype(BF16), v_ref[...],
                                                                  preferred_element_type=F32)
            m_ref[rows, :] = m_new

    @pl.when(ik < iq)
    def _():
        run(False)

    @pl.when(ik == iq)
    def _():
        run(True)
        o_ref[...] = (acc_ref[...] / l_ref[...]).astype(o_ref.dtype)


def fox_attention(q, kt, v, ck_t, bsz, seq, heads):
    t, d = q.shape
    hd = FOX_HEAD_DIM
    assert hd == LANES
    tb = _pick(seq, 1024)
    nb = seq // tb
    pairs = [(iq, ik) for iq in range(nb) for ik in range(iq + 1)]
    iq_tab = jnp.asarray([pr[0] for pr in pairs], jnp.int32)
    ik_tab = jnp.asarray([pr[1] for pr in pairs], jnp.int32)
    grid_spec = pltpu.PrefetchScalarGridSpec(
        num_scalar_prefetch=2,
        grid=(bsz, heads, len(pairs)),
        in_specs=[
            pl.BlockSpec((tb, hd), lambda b, h, s, iqt, ikt: (b * nb + iqt[s], h)),
            pl.BlockSpec((hd, tb), lambda b, h, s, iqt, ikt: (h, b * nb + ikt[s])),
            pl.BlockSpec((tb, hd), lambda b, h, s, iqt, ikt: (b * nb + ikt[s], h)),
            pl.BlockSpec((None, None, 1, tb), lambda b, h, s, iqt, ikt: (b, h, 0, ikt[s])),
        ],
        out_specs=pl.BlockSpec((tb, hd), lambda b, h, s, iqt, ikt: (b * nb + iqt[s], h)),
        scratch_shapes=[pltpu.VMEM((tb, LANES), F32), pltpu.VMEM((tb, LANES), F32), pltpu.VMEM((tb, hd), F32)],
    )
    return pl.pallas_call(
        functools.partial(_fox_attn_body, tb=tb, rb=_pick(tb, 128)),
        out_shape=jax.ShapeDtypeStruct((t, d), BF16),
        grid_spec=grid_spec,
        compiler_params=_params("parallel", "parallel", "arbitrary"),
        name="fox_attention",
    )(iq_tab, ik_tab, q, kt, v, ck_t)


def _top_rows(blocks, k, with_rank=False):
    out = []
    ranks = [jnp.full(w.shape, float(k), F32) for w in blocks] if with_rank else None
    for r in range(k):
        m = functools.reduce(jnp.maximum, blocks)
        m = jnp.max(m, axis=0, keepdims=True)
        out.append(m)
        hit = [w == m for w in blocks]
        if with_rank:
            ranks = [jnp.where(e, float(r), rk) for e, rk in zip(hit, ranks)]
        blocks = [jnp.where(e, NEG_INF, w) for e, w in zip(hit, blocks)]
    return (out, ranks) if with_rank else out


def _bf16_pair_words(x):
    u = pltpu.bitcast(x, jnp.uint32)
    lsb = lax.shift_right_logical(u, jnp.uint32(16)) & jnp.uint32(1)
    r = lax.shift_right_logical(u + jnp.uint32(0x7FFF) + lsb, jnp.uint32(16))
    return r | lax.shift_left(r, jnp.uint32(16))


def _peer_route_body(qt_ref, keys_ref, rows_ref, r2_ref, e2_ref, *, nkeys, kd):
    k = PEER_TOPK
    ni = PEER_KEYS_PER_STEP
    rb = SUBLANES
    for h in range(PEER_HEADS):
        base = h * 2 * kd
        s1 = jnp.dot(keys_ref[0], qt_ref[base:base + kd, :], preferred_element_type=F32)
        s2 = jnp.dot(keys_ref[1], qt_ref[base + kd:base + 2 * kd, :], preferred_element_type=F32)
        v1 = jnp.concatenate(_top_rows([s1[r:r + rb] for r in range(0, nkeys, rb)], k), axis=0)
        v2_rows, rank2 = _top_rows([s2[r:r + rb] for r in range(0, nkeys, rb)], k, with_rank=True)
        v2 = jnp.concatenate(v2_rows, axis=0)
        half = k // 2
        cands = [v1[0:1] + v2[0:half], v1[0:1] + v2[half:k]]
        cands += [v1[a:a + 1] + v2[0:half] for a in range(1, half)]
        cands += [v1[half:k] + v2[0:1]]
        sc = _top_rows(cands, k)
        m, tau = sc[0], sc[k - 1]
        z = jnp.zeros_like(m)
        for r in sc:
            z = z + jnp.exp(r - m)
        e1 = jnp.exp(s1 - v1[0:1]) / z
        cnt = jnp.zeros_like(s1)
        for b in range(k):
            cnt = cnt + jnp.where(s1 + v2_rows[b] >= tau, 1.0, 0.0)
        cnt_w, e1_w = _bf16_pair_words(cnt), _bf16_pair_words(e1)
        for blk in range(nkeys // ni):
            rows_ref[h, blk, 0:ni, :] = cnt_w[blk * ni:(blk + 1) * ni]
            rows_ref[h, blk, ni:2 * ni, :] = e1_w[blk * ni:(blk + 1) * ni]
        half_keys = nkeys // 2
        r2 = jnp.concatenate(rank2, axis=0).astype(BF16)
        e2 = jnp.exp(s2 - v2[0:1]).astype(BF16)
        r2_ref[h * half_keys:(h + 1) * half_keys, :] = pltpu.bitcast(r2, jnp.uint32)
        e2_ref[h * half_keys:(h + 1) * half_keys, :] = pltpu.bitcast(e2, jnp.uint32)


def peer_route(qt, keys):
    rows, t = qt.shape
    _, nkeys, kd = keys.shape
    ni = PEER_KEYS_PER_STEP
    tm = _pick(t, 256)
    big = jax.ShapeDtypeStruct((PEER_HEADS * nkeys // 2, t), jnp.uint32)
    big_spec = pl.BlockSpec((PEER_HEADS * nkeys // 2, tm), lambda i: (0, i))
    return pl.pallas_call(
        functools.partial(_peer_route_body, nkeys=nkeys, kd=kd),
        out_shape=(jax.ShapeDtypeStruct((PEER_HEADS, nkeys // ni, 2 * ni, t), jnp.uint32), big, big),
        grid=(t // tm,),
        in_specs=[pl.BlockSpec((rows, tm), lambda i: (0, i)),
                  pl.BlockSpec((2, nkeys, kd), lambda i: (0, 0, 0))],
        out_specs=(pl.BlockSpec((PEER_HEADS, nkeys // ni, 2 * ni, tm), lambda i: (0, 0, 0, i)),
                   big_spec, big_spec),
        compiler_params=_params("parallel"),
        name="peer_route",
    )(qt, keys)


def _gelu(x):
    return 0.5 * x * (1.0 + lax.erf(x * (1.0 / math.sqrt(2.0))))


def _peer_expert_body(xt_ref, u_ref, vt_ref, rows_ref, r2_ref, e2_ref, o_ref, act_ref, w_ref, *, nkeys, tm):
    e = pl.program_id(1)
    ni = PEER_KEYS_PER_STEP

    @pl.when(e == 0)
    def _():
        o_ref[...] = jnp.zeros_like(o_ref)
        act_ref[...] = jnp.zeros_like(act_ref)

    pack = 2 * SUBLANES
    half_keys = nkeys // 2

    def packed_rows(word_row):
        tile = pltpu.bitcast(jnp.broadcast_to(word_row, (SUBLANES, LANES)), BF16)
        return jnp.tile(tile, (nkeys // pack, 1))

    for ii in range(ni):
        rows = slice(ii * nkeys, (ii + 1) * nkeys)
        for sub in range(tm // LANES):
            cols = slice(sub * LANES, (sub + 1) * LANES)
            gate = jnp.zeros((nkeys, LANES), BF16)
            for h in range(PEER_HEADS):
                cnt = packed_rows(rows_ref[h, ii:ii + 1, cols])
                e1 = packed_rows(rows_ref[h, ni + ii:ni + ii + 1, cols])
                keys = slice(h * half_keys, (h + 1) * half_keys)
                r2 = pltpu.bitcast(r2_ref[keys, cols], BF16)
                e2 = pltpu.bitcast(e2_ref[keys, cols], BF16)
                gate = gate + jnp.where(r2 < cnt, e2, jnp.zeros((), BF16)) * e1
            w_ref[rows, cols] = (gate.astype(F32) * _gelu(act_ref[rows, cols])).astype(BF16)

    act_ref[...] = jnp.dot(u_ref[...], xt_ref[...], preferred_element_type=F32)
    o_ref[...] += jnp.dot(vt_ref[...], w_ref[...], preferred_element_type=F32)


def peer_experts(xt, u, vt, rows, r2, e2):
    d, t = xt.shape
    n_exp = u.shape[0]
    nkeys = 2 * r2.shape[0] // PEER_HEADS
    ni = PEER_KEYS_PER_STEP
    tm = _pick(t, 512)
    eb = ni * nkeys
    n_e = n_exp // eb
    once = dict(pipeline_mode=pl.Buffered(1))
    cur = lambda e: jnp.minimum(e, n_e - 1)
    prev = lambda e: jnp.maximum(e - 1, 0)
    route_spec = lambda: pl.BlockSpec((PEER_HEADS * nkeys // 2, tm), lambda i, e: (0, i), **once)
    return pl.pallas_call(
        functools.partial(_peer_expert_body, nkeys=nkeys, tm=tm),
        out_shape=jax.ShapeDtypeStruct((d, t), F32),
        grid=(t // tm, n_e + 1),
        in_specs=[pl.BlockSpec((d, tm), lambda i, e: (0, i), **once),
                  pl.BlockSpec((eb, d), lambda i, e: (cur(e), 0)),
                  pl.BlockSpec((d, eb), lambda i, e: (0, prev(e))),
                  pl.BlockSpec((PEER_HEADS, None, 2 * ni, tm), lambda i, e: (0, prev(e), 0, i)),
                  route_spec(), route_spec()],
        out_specs=pl.BlockSpec((d, tm), lambda i, e: (0, i)),
        scratch_shapes=[pltpu.VMEM((eb, tm), F32), pltpu.VMEM((eb, tm), BF16)],
        compiler_params=_params("parallel", "arbitrary"),
        name="peer_experts",
    )(xt, u, vt, rows, r2, e2)


def _peer_ffn_t(xt, w_q, subkeys, u, v):
    qt = matmul(w_q.T.astype(BF16), xt, BF16, name="peer_query")
    rows, r2, e2 = peer_route(qt, subkeys.astype(BF16))
    return peer_experts(xt, u.astype(BF16), v.T.astype(BF16), rows, r2, e2)


def kernel(x, p, ln_mix, ssd_w_in, ssd_conv_w, ssd_conv_b, ssd_dt_bias, ssd_a_log, ssd_d_skip, ssd_norm_w, ssd_w_out, fox_w_in, fox_b_f, fox_w_out, ln_ffn, peer_w_q, peer_subkeys, peer_u, peer_v, ln_ple, ple_w_gate, ple_w_proj, final_norm):
    bsz, seq, d = x.shape
    depth = p.shape[0]
    t = bsz * seq
    h = x.reshape(t, d)
    for i in range(depth):
        j = i // 2
        if i % 2 == 0:
            (hn,) = rmsnorm(h, ln_mix[i])
            d_inner = ssd_w_out.shape[1]
            conv_dim = ssd_conv_w.shape[2]
            w_in = ssd_w_in[j]
            zxbc = matmul(hn, w_in[:, :d_inner + conv_dim].astype(BF16), BF16, name="ssd_in_proj")
            dt_raw = matmul(hn, w_in[:, d_inner + conv_dim:].astype(BF16), F32, name="ssd_dt_proj")
            y = ssd_core(zxbc, dt_raw, ssd_conv_w[j], ssd_conv_b[j], ssd_dt_bias[j], ssd_a_log[j],
                         ssd_d_skip[j], ssd_norm_w[j], bsz, seq)
            h = matmul(y, ssd_w_out[j].astype(BF16), F32, epilogue=_epi_residual, extra=(h,),
                       name="ssd_out_proj")
        else:
            hn, hnt = rmsnorm(h, ln_mix[i], emit_t=True)
            heads = fox_b_f.shape[1]
            w_in = fox_w_in[j]
            qscale = FOX_HEAD_DIM ** -0.5 * LOG2E
            q = matmul(hn, (w_in[:, :d] * qscale).astype(BF16), BF16, name="fox_q_proj")
            kt = matmul(w_in[:, d:2 * d].T.astype(BF16), hnt, BF16, name="fox_k_proj")
            v = matmul(hn, w_in[:, 2 * d:3 * d].astype(BF16), BF16, name="fox_v_proj")
            w_f = jnp.pad(w_in[:, 3 * d:], ((0, 0), (0, LANES - heads))).astype(BF16)
            f_raw = matmul(hn, w_f, F32, name="fox_gate_proj")
            b_f = jnp.pad(fox_b_f[j].astype(F32), (0, LANES - heads)).reshape(1, LANES)
            cum = fox_cum(f_raw, b_f, bsz, seq)
            ck_t = cum.reshape(bsz, seq, LANES)[:, :, :heads].transpose(0, 2, 1).reshape(bsz, heads, 1, seq)
            o = fox_attention(q, kt, v, ck_t, bsz, seq, heads)
            h = matmul(o, fox_w_out[j].astype(BF16), F32, epilogue=_epi_residual, extra=(h,),
                       name="fox_out_proj")
        (hnt,) = rmsnorm(h, ln_ffn[i], emit_n=False, emit_t=True)
        yt = _peer_ffn_t(hnt, peer_w_q[i], peer_subkeys[i], peer_u[i], peer_v[i])
        h, hn = rmsnorm(h, ln_ple[i], add=yt, add_transposed=True)
        p_i = p[i].reshape(t, -1).astype(BF16)
        w_proj = ple_w_proj[i].astype(BF16)
        pdim = p_i.shape[1]
        h = matmul(hn, ple_w_gate[i].astype(BF16), F32, epilogue=_epi_ple, extra=(h, p_i, w_proj),
                   extra_specs=lambda tm, tn: [pl.BlockSpec((tm, tn), lambda a, b, c: (a, b)),
                                               pl.BlockSpec((tm, pdim), lambda a, b, c: (a, 0)),
                                               pl.BlockSpec((pdim, tn), lambda a, b, c: (0, b))],
                   name="ple")
    (out,) = rmsnorm(h, final_norm, out_dtype=F32)
    return out.reshape(bsz, seq, d)
```

```python
import functools
import math

import jax
import jax.numpy as jnp
from jax import lax
from jax.experimental import pallas as pl
from jax.experimental.pallas import tpu as pltpu

F32 = jnp.float32
BF16 = jnp.bfloat16
NEG_INF = float("-inf")
LOG2E = math.log2(math.e)

NORM_EPS = 1e-6
SSD_GROUPS = 8
SSD_HEAD_DIM = 64
SSD_CHUNK = 128
SSD_CONV = 4
FOX_HEAD_DIM = 128
PEER_HEADS = 8
PEER_TOPK = 16
PEER_KEYS_PER_STEP = 4

V7X_VMEM_BYTES = 64 * 1024 * 1024
VMEM_LIMIT = V7X_VMEM_BYTES - 8 * 1024 * 1024
LANES = 128
SUBLANES = 8


def _params(*sem):
    return pltpu.CompilerParams(dimension_semantics=sem, vmem_limit_bytes=VMEM_LIMIT)


def _pick(n, pref):
    t = min(n, pref)
    while n % t:
        t //= 2
    return t


def _norm_rows(x, w_ref):
    return x * lax.rsqrt(jnp.mean(x * x, axis=-1, keepdims=True) + NORM_EPS) * w_ref[...]


def _rmsnorm_body(*refs, has_add, emit_n, emit_t, add_transposed):
    refs = list(refs)
    x = refs.pop(0)[...].astype(F32)
    if has_add:
        y = refs.pop(0)[...]
        x = x + (y.T if add_transposed else y)
    w_ref = refs.pop(0)
    if has_add:
        refs.pop(0)[...] = x
    n = _norm_rows(x, w_ref)
    if emit_n:
        o = refs.pop(0)
        o[...] = n.astype(o.dtype)
    if emit_t:
        o = refs.pop(0)
        o[...] = n.T.astype(o.dtype)


def rmsnorm(x, w, out_dtype=BF16, add=None, add_transposed=False, emit_n=True, emit_t=False):
    t, d = x.shape
    tm = _pick(t, 256)
    row = pl.BlockSpec((tm, d), lambda i: (i, 0))
    col = pl.BlockSpec((d, tm), lambda i: (0, i))
    in_specs, args = [row], [x]
    if add is not None:
        in_specs.append(col if add_transposed else row)
        args.append(add)
    in_specs.append(pl.BlockSpec((1, d), lambda i: (0, 0)))
    args.append(w.reshape(1, d).astype(F32))
    out_shape, out_specs = [], []
    if add is not None:
        out_shape.append(jax.ShapeDtypeStruct((t, d), F32))
        out_specs.append(row)
    if emit_n:
        out_shape.append(jax.ShapeDtypeStruct((t, d), out_dtype))
        out_specs.append(row)
    if emit_t:
        out_shape.append(jax.ShapeDtypeStruct((d, t), out_dtype))
        out_specs.append(col)
    return pl.pallas_call(
        functools.partial(_rmsnorm_body, has_add=add is not None, emit_n=emit_n, emit_t=emit_t,
                          add_transposed=add_transposed),
        out_shape=tuple(out_shape),
        grid=(t // tm,),
        in_specs=in_specs,
        out_specs=tuple(out_specs),
        compiler_params=_params("parallel"),
        name="rmsnorm",
    )(*args)


def _matmul_body(*refs, nk, epilogue, n_extra):
    a_ref, b_ref = refs[0], refs[1]
    extra = refs[2:2 + n_extra]
    o_ref = refs[2 + n_extra]
    if nk == 1:
        acc = jnp.dot(a_ref[...], b_ref[...], preferred_element_type=F32)
        o_ref[...] = epilogue(acc, *extra).astype(o_ref.dtype)
        return
    acc_ref = refs[3 + n_extra]
    k = pl.program_id(2)

    @pl.when(k == 0)
    def _():
        acc_ref[...] = jnp.dot(a_ref[...], b_ref[...], preferred_element_type=F32)

    @pl.when(jnp.logical_and(k > 0, k < nk - 1))
    def _():
        acc_ref[...] += jnp.dot(a_ref[...], b_ref[...], preferred_element_type=F32)

    @pl.when(k == nk - 1)
    def _():
        acc = acc_ref[...] + jnp.dot(a_ref[...], b_ref[...], preferred_element_type=F32)
        o_ref[...] = epilogue(acc, *extra).astype(o_ref.dtype)


def _epi_plain(acc):
    return acc


def _epi_residual(acc, res_ref):
    return res_ref[...] + acc


def _epi_ple(acc, h_ref, p_ref, wp_ref):
    proj = jnp.dot(p_ref[...], wp_ref[...], preferred_element_type=F32)
    return h_ref[...] + proj * jax.nn.sigmoid(acc)


def matmul(a, b, out_dtype, epilogue=_epi_plain, extra=(), extra_specs=None,
           tm=1024, tn=1024, tk=2048, name="matmul"):
    m, kd = a.shape
    _, n = b.shape
    tm, tn, tk = _pick(m, tm), _pick(n, tn), _pick(kd, tk)
    nk = kd // tk
    if extra_specs is None:
        extra_specs = [pl.BlockSpec((tm, tn), lambda i, j, k: (i, j)) for _ in extra]
    else:
        extra_specs = extra_specs(tm, tn)
    return pl.pallas_call(
        functools.partial(_matmul_body, nk=nk, epilogue=epilogue, n_extra=len(extra)),
        out_shape=jax.ShapeDtypeStruct((m, n), out_dtype),
        grid=(m // tm, n // tn, nk),
        in_specs=[pl.BlockSpec((tm, tk), lambda i, j, k: (i, k)),
                  pl.BlockSpec((tk, tn), lambda i, j, k: (k, j))] + list(extra_specs),
        out_specs=pl.BlockSpec((tm, tn), lambda i, j, k: (i, j)),
        scratch_shapes=[pltpu.VMEM((tm, tn), F32)] if nk > 1 else [],
        compiler_params=_params("parallel", "parallel", "arbitrary"),
        name=name,
    )(a, b, *extra)


def _silu(x):
    return x * jax.nn.sigmoid(x)


def _ssd_body(z_ref, x_ref, b_ref, c_ref, dt_ref, dtt_ref, bias_ref, biast_ref, alog_ref, alogt_ref,
              wx_ref, wb_ref, wc_ref, bx_ref, bb_ref, bc_ref, dskip_ref, normw_ref, expand_ref,
              o_ref, state_ref, xpad_ref, bpad_ref, cpad_ref, *, q, hg, p):
    c_idx = pl.program_id(2)
    halo = SUBLANES

    @pl.when(c_idx == 0)
    def _():
        state_ref[...] = jnp.zeros_like(state_ref)
        xpad_ref[0:halo, :] = jnp.zeros((halo, xpad_ref.shape[1]), F32)
        bpad_ref[0:halo, :] = jnp.zeros((halo, bpad_ref.shape[1]), F32)
        cpad_ref[0:halo, :] = jnp.zeros((halo, cpad_ref.shape[1]), F32)

    def conv_silu(pad_ref, cur_ref, w_ref, bias_row_ref):
        pad_ref[halo:halo + q, :] = cur_ref[...].astype(F32)
        acc = bias_row_ref[...]
        for k in range(SSD_CONV):
            off = halo - (SSD_CONV - 1) + k
            acc = acc + w_ref[k:k + 1, :] * pad_ref[off:off + q, :]
        pad_ref[0:halo, :] = pad_ref[q:q + halo, :]
        return _silu(acc)

    xs = conv_silu(xpad_ref, x_ref, wx_ref, bx_ref)
    bm = conv_silu(bpad_ref, b_ref, wb_ref, bb_ref)
    cm = conv_silu(cpad_ref, c_ref, wc_ref, bc_ref)

    dt = jax.nn.softplus(dt_ref[...] + bias_ref[...])
    dtt = jax.nn.softplus(dtt_ref[...] + biast_ref[...])
    a = dt * (-jnp.exp(alog_ref[...]))
    at = dtt * (-jnp.exp(alogt_ref[...]))
    row = lax.broadcasted_iota(jnp.int32, (q, q), 0)
    col = lax.broadcasted_iota(jnp.int32, (q, q), 1)
    causal = row >= col
    tri = causal.astype(F32)
    hi = lax.Precision.HIGHEST
    a_cs = jnp.dot(tri, a, precision=hi, preferred_element_type=F32)
    a_cst = jnp.dot(at, (row <= col).astype(F32), precision=hi, preferred_element_type=F32)

    a_last = a_cs[q - 1:q, :]
    small = jnp.concatenate([dt, jnp.exp(a_cs), jnp.exp(a_last - a_cs)], axis=0)
    small_hi = small.astype(BF16)
    small_lo = (small - small_hi.astype(F32)).astype(BF16)
    full = (jnp.dot(small_hi, expand_ref[...], preferred_element_type=F32)
            + jnp.dot(small_lo, expand_ref[...], preferred_element_type=F32))
    dt_full, dec_full, decay_full = full[0:q], full[q:2 * q], full[2 * q:3 * q]

    x_dt = xs * dt_full
    x_dt_b = x_dt.astype(BF16)
    bm_b = bm.astype(BF16)
    cm_b = cm.astype(BF16)

    cb = lax.dot_general(cm_b, bm_b, (((1,), (1,)), ((), ())), preferred_element_type=F32)

    state = state_ref[...]
    y = jnp.dot(cm_b, state.astype(BF16), preferred_element_type=F32) * dec_full

    lane = lax.broadcasted_iota(jnp.int32, (q, 2 * p), 1)
    y_parts = []
    for pair in range(hg // 2):
        ms = []
        for h in (2 * pair, 2 * pair + 1):
            seg = a_cs[:, h:h + 1] - a_cst[h:h + 1, :]
            ms.append((cb * jnp.exp(jnp.where(causal, seg, NEG_INF))).astype(BF16))
        lhs = jnp.concatenate(ms, axis=1)
        xp = x_dt_b[:, 2 * pair * p:(2 * pair + 2) * p]
        zero = jnp.zeros_like(xp)
        rhs = jnp.concatenate([jnp.where(lane < p, xp, zero), jnp.where(lane >= p, xp, zero)], axis=0)
        y_parts.append(jnp.dot(lhs, rhs, preferred_element_type=F32))
    y = y + jnp.concatenate(y_parts, axis=1)

    xd = (x_dt * decay_full).astype(BF16)
    upd = lax.dot_general(bm_b, xd, (((0,), (0,)), ((), ())), preferred_element_type=F32)
    state_ref[...] = state * dec_full[q - 1:q, :] + upd

    y = y + dskip_ref[...] * xs
    g = y * _silu(z_ref[...].astype(F32))
    g = g * lax.rsqrt(jnp.mean(g * g, axis=-1, keepdims=True) + NORM_EPS)
    o_ref[...] = (g * normw_ref[...]).astype(o_ref.dtype)


def ssd_core(zxbc, dt_raw, conv_w, conv_b, dt_bias, a_log, d_skip, norm_w, bsz, seq):
    t = bsz * seq
    heads = dt_bias.shape[0]
    g = SSD_GROUPS
    hg = heads // g
    p = SSD_HEAD_DIM
    gw = hg * p
    d_inner = heads * p
    n = (zxbc.shape[1] - 2 * d_inner) // (2 * g)
    q = SSD_CHUNK
    nc = seq // q
    assert gw % LANES == 0 and n % LANES == 0 and hg % 2 == 0 and 2 * p == LANES

    dtg = dt_raw.reshape(bsz, seq, g, hg).transpose(0, 2, 1, 3)
    dtgt = dtg.transpose(0, 1, 3, 2)
    bias = dt_bias.reshape(g, 1, hg).astype(F32)
    biast = dt_bias.reshape(g, hg, 1).astype(F32)
    alog = a_log.reshape(g, 1, hg).astype(F32)
    alogt = a_log.reshape(g, hg, 1).astype(F32)
    conv_w = conv_w.astype(F32)
    conv_b = conv_b.reshape(1, -1).astype(F32)
    dskip_full = jnp.repeat(d_skip.astype(F32), p).reshape(1, d_inner)
    normw = norm_w.reshape(1, d_inner).astype(F32)
    expand = (jnp.arange(gw)[None, :] // p == jnp.arange(hg)[:, None]).astype(BF16)

    xblk, bblk, cblk = d_inner // gw, 2 * d_inner // n, (2 * d_inner + g * n) // n
    wx_blk, wb_blk, wc_blk = 0, d_inner // n, (d_inner + g * n) // n
    rows = lambda b, gi, c: b * nc + c
    in_specs = [
        pl.BlockSpec((q, gw), lambda b, gi, c: (rows(b, gi, c), gi)),
        pl.BlockSpec((q, gw), lambda b, gi, c: (rows(b, gi, c), xblk + gi)),
        pl.BlockSpec((q, n), lambda b, gi, c: (rows(b, gi, c), bblk + gi)),
        pl.BlockSpec((q, n), lambda b, gi, c: (rows(b, gi, c), cblk + gi)),
        pl.BlockSpec((None, None, q, hg), lambda b, gi, c: (b, gi, c, 0)),
        pl.BlockSpec((None, None, hg, q), lambda b, gi, c: (b, gi, 0, c)),
        pl.BlockSpec((None, 1, hg), lambda b, gi, c: (gi, 0, 0)),
        pl.BlockSpec((None, hg, 1), lambda b, gi, c: (gi, 0, 0)),
        pl.BlockSpec((None, 1, hg), lambda b, gi, c: (gi, 0, 0)),
        pl.BlockSpec((None, hg, 1), lambda b, gi, c: (gi, 0, 0)),
        pl.BlockSpec((SSD_CONV, gw), lambda b, gi, c: (0, wx_blk + gi)),
        pl.BlockSpec((SSD_CONV, n), lambda b, gi, c: (0, wb_blk + gi)),
        pl.BlockSpec((SSD_CONV, n), lambda b, gi, c: (0, wc_blk + gi)),
        pl.BlockSpec((1, gw), lambda b, gi, c: (0, wx_blk + gi)),
        pl.BlockSpec((1, n), lambda b, gi, c: (0, wb_blk + gi)),
        pl.BlockSpec((1, n), lambda b, gi, c: (0, wc_blk + gi)),
        pl.BlockSpec((1, gw), lambda b, gi, c: (0, gi)),
        pl.BlockSpec((1, gw), lambda b, gi, c: (0, gi)),
        pl.BlockSpec((hg, gw), lambda b, gi, c: (0, 0)),
    ]
    return pl.pallas_call(
        functools.partial(_ssd_body, q=q, hg=hg, p=p),
        out_shape=jax.ShapeDtypeStruct((t, d_inner), BF16),
        grid=(bsz, g, nc),
        in_specs=in_specs,
        out_specs=pl.BlockSpec((q, gw), lambda b, gi, c: (rows(b, gi, c), gi)),
        scratch_shapes=[pltpu.VMEM((n, gw), F32),
                        pltpu.VMEM((q + SUBLANES, gw), F32),
                        pltpu.VMEM((q + SUBLANES, n), F32),
                        pltpu.VMEM((q + SUBLANES, n), F32)],
        compiler_params=_params("parallel", "parallel", "arbitrary"),
        name="ssd_core",
    )(zxbc, zxbc, zxbc, zxbc, dtg, dtgt, bias, biast, alog, alogt,
      conv_w, conv_w, conv_w, conv_b, conv_b, conv_b, dskip_full, normw, expand)


def _fox_cum_body(f_ref, bf_ref, o_ref, carry_ref, *, tl):
    @pl.when(pl.program_id(1) == 0)
    def _():
        carry_ref[...] = jnp.zeros_like(carry_ref)

    logf = jax.nn.log_sigmoid(f_ref[...] + bf_ref[...])
    row = lax.broadcasted_iota(jnp.int32, (tl, tl), 0)
    col = lax.broadcasted_iota(jnp.int32, (tl, tl), 1)
    tri = (row >= col).astype(F32)
    cum = jnp.dot(tri, logf, precision=lax.Precision.HIGHEST, preferred_element_type=F32) + carry_ref[...]
    o_ref[...] = cum * LOG2E
    carry_ref[...] = cum[tl - 1:tl, :]


def fox_cum(f_raw, b_f, bsz, seq):
    t, w = f_raw.shape
    tl = _pick(seq, 256)
    nl = seq // tl
    return pl.pallas_call(
        functools.partial(_fox_cum_body, tl=tl),
        out_shape=jax.ShapeDtypeStruct((t, w), F32),
        grid=(bsz, nl),
        in_specs=[pl.BlockSpec((tl, w), lambda b, i: (b * nl + i, 0)),
                  pl.BlockSpec((1, w), lambda b, i: (0, 0))],
        out_specs=pl.BlockSpec((tl, w), lambda b, i: (b * nl + i, 0)),
        scratch_shapes=[pltpu.VMEM((1, w), F32)],
        compiler_params=_params("parallel", "arbitrary"),
        name="fox_cum",
    )(f_raw, b_f)


def _fox_attn_body(iq_tab, ik_tab, q_ref, kt_ref, v_ref, ck_ref, o_ref, m_ref, l_ref, acc_ref, *, tb, rb):
    step = pl.program_id(2)
    iq = iq_tab[step]
    ik = ik_tab[step]

    @pl.when(ik == 0)
    def _():
        m_ref[...] = jnp.full_like(m_ref, NEG_INF)
        l_ref[...] = jnp.zeros_like(l_ref)
        acc_ref[...] = jnp.zeros_like(acc_ref)

    def run(masked):
        def width(r0):
            return r0 + rb if masked else tb

        def logits(r0):
            return jnp.dot(q_ref[r0:r0 + rb, :], kt_ref[:, 0:width(r0)], preferred_element_type=F32)

        starts = list(range(0, tb, rb))
        ahead = 2
        pending = [logits(r0) for r0 in starts[:ahead]]
        for idx, r0 in enumerate(starts):
            rows = slice(r0, r0 + rb)
            wd = width(r0)
            if idx + ahead < len(starts):
                pending.append(logits(starts[idx + ahead]))
            s = pending.pop(0) - ck_ref[:, 0:wd]
            if masked:
                qpos = r0 + lax.broadcasted_iota(jnp.int32, (rb, wd), 0)
                kpos = lax.broadcasted_iota(jnp.int32, (rb, wd), 1)
                s = jnp.where(kpos <= qpos, s, NEG_INF)
            m_prev = m_ref[rows, :]
            m_new = jnp.maximum(m_prev, jnp.max(s, axis=1, keepdims=True))
            alpha = jnp.exp2(m_prev - m_new)
            pr = jnp.exp2(s - jnp.tile(m_new, (1, wd // LANES)))
            l_ref[rows, :] = alpha * l_ref[rows, :] + jnp.sum(pr, axis=1, keepdims=True)
            acc_ref[rows, :] = alpha * acc_ref[rows, :] + jnp.dot(pr.astype(BF16), v_ref[0:wd, :],
                                                                  preferred_element_type=F32)
            m_ref[rows, :] = m_new

    @pl.when(ik < iq)
    def _():
        run(False)

    @pl.when(ik == iq)
    def _():
        run(True)
        o_ref[...] = (acc_ref[...] / l_ref[...]).astype(o_ref.dtype)


def fox_attention(q, kt, v, ck_t, bsz, seq, heads):
    t, d = q.shape
    hd = FOX_HEAD_DIM
    assert hd == LANES
    tb = _pick(seq, 1024)
    nb = seq // tb
    pairs = [(iq, ik) for iq in range(nb) for ik in range(iq + 1)]
    iq_tab = jnp.asarray([pr[0] for pr in pairs], jnp.int32)
    ik_tab = jnp.asarray([pr[1] for pr in pairs], jnp.int32)
    grid_spec = pltpu.PrefetchScalarGridSpec(
        num_scalar_prefetch=2,
        grid=(bsz, heads, len(pairs)),
        in_specs=[
            pl.BlockSpec((tb, hd), lambda b, h, s, iqt, ikt: (b * nb + iqt[s], h)),
            pl.BlockSpec((hd, tb), lambda b, h, s, iqt, ikt: (h, b * nb + ikt[s])),
            pl.BlockSpec((tb, hd), lambda b, h, s, iqt, ikt: (b * nb + ikt[s], h)),
            pl.BlockSpec((None, None, 1, tb), lambda b, h, s, iqt, ikt: (b, h, 0, ikt[s])),
        ],
        out_specs=pl.BlockSpec((tb, hd), lambda b, h, s, iqt, ikt: (b * nb + iqt[s], h)),
        scratch_shapes=[pltpu.VMEM((tb, LANES), F32), pltpu.VMEM((tb, LANES), F32), pltpu.VMEM((tb, hd), F32)],
    )
    return pl.pallas_call(
        functools.partial(_fox_attn_body, tb=tb, rb=_pick(tb, 128)),
        out_shape=jax.ShapeDtypeStruct((t, d), BF16),
        grid_spec=grid_spec,
        compiler_params=_params("parallel", "parallel", "arbitrary"),
        name="fox_attention",
    )(iq_tab, ik_tab, q, kt, v, ck_t)


def _top_rows(blocks, k, with_rank=False):
    out = []
    ranks = [jnp.full(w.shape, float(k), F32) for w in blocks] if with_rank else None
    for r in range(k):
        m = functools.reduce(jnp.maximum, blocks)
        m = jnp.max(m, axis=0, keepdims=True)
        out.append(m)
        hit = [w == m for w in blocks]
        if with_rank:
            ranks = [jnp.where(e, float(r), rk) for e, rk in zip(hit, ranks)]
        blocks = [jnp.where(e, NEG_INF, w) for e, w in zip(hit, blocks)]
    return (out, ranks) if with_rank else out


def _bf16_pair_words(x):
    u = pltpu.bitcast(x, jnp.uint32)
    lsb = lax.shift_right_logical(u, jnp.uint32(16)) & jnp.uint32(1)
    r = lax.shift_right_logical(u + jnp.uint32(0x7FFF) + lsb, jnp.uint32(16))
    return r | lax.shift_left(r, jnp.uint32(16))


def _peer_route_body(qt_ref, keys_ref, rows_ref, r2_ref, e2_ref, *, nkeys, kd):
    k = PEER_TOPK
    ni = PEER_KEYS_PER_STEP
    rb = SUBLANES
    for h in range(PEER_HEADS):
        base = h * 2 * kd
        s1 = jnp.dot(keys_ref[0], qt_ref[base:base + kd, :], preferred_element_type=F32)
        s2 = jnp.dot(keys_ref[1], qt_ref[base + kd:base + 2 * kd, :], preferred_element_type=F32)
        v1 = jnp.concatenate(_top_rows([s1[r:r + rb] for r in range(0, nkeys, rb)], k), axis=0)
        v2_rows, rank2 = _top_rows([s2[r:r + rb] for r in range(0, nkeys, rb)], k, with_rank=True)
        v2 = jnp.concatenate(v2_rows, axis=0)
        half = k // 2
        cands = [v1[0:1] + v2[0:half], v1[0:1] + v2[half:k]]
        cands += [v1[a:a + 1] + v2[0:half] for a in range(1, half)]
        cands += [v1[half:k] + v2[0:1]]
        sc = _top_rows(cands, k)
        m, tau = sc[0], sc[k - 1]
        z = jnp.zeros_like(m)
        for r in sc:
            z = z + jnp.exp(r - m)
        e1 = jnp.exp(s1 - v1[0:1]) / z
        cnt = jnp.zeros_like(s1)
        for b in range(k):
            cnt = cnt + jnp.where(s1 + v2_rows[b] >= tau, 1.0, 0.0)
        cnt_w, e1_w = _bf16_pair_words(cnt), _bf16_pair_words(e1)
        for blk in range(nkeys // ni):
            rows_ref[h, blk, 0:ni, :] = cnt_w[blk * ni:(blk + 1) * ni]
            rows_ref[h, blk, ni:2 * ni, :] = e1_w[blk * ni:(blk + 1) * ni]
        half_keys = nkeys // 2
        r2 = jnp.concatenate(rank2, axis=0).astype(BF16)
        e2 = jnp.exp(s2 - v2[0:1]).astype(BF16)
        r2_ref[h * half_keys:(h + 1) * half_keys, :] = pltpu.bitcast(r2, jnp.uint32)
        e2_ref[h * half_keys:(h + 1) * half_keys, :] = pltpu.bitcast(e2, jnp.uint32)


def peer_route(qt, keys):
    rows, t = qt.shape
    _, nkeys, kd = keys.shape
    ni = PEER_KEYS_PER_STEP
    tm = _pick(t, 256)
    big = jax.ShapeDtypeStruct((PEER_HEADS * nkeys // 2, t), jnp.uint32)
    big_spec = pl.BlockSpec((PEER_HEADS * nkeys // 2, tm), lambda i: (0, i))
    return pl.pallas_call(
        functools.partial(_peer_route_body, nkeys=nkeys, kd=kd),
        out_shape=(jax.ShapeDtypeStruct((PEER_HEADS, nkeys // ni, 2 * ni, t), jnp.uint32), big, big),
        grid=(t // tm,),
        in_specs=[pl.BlockSpec((rows, tm), lambda i: (0, i)),
                  pl.BlockSpec((2, nkeys, kd), lambda i: (0, 0, 0))],
        out_specs=(pl.BlockSpec((PEER_HEADS, nkeys // ni, 2 * ni, tm), lambda i: (0, 0, 0, i)),
                   big_spec, big_spec),
        compiler_params=_params("parallel"),
        name="peer_route",
    )(qt, keys)


def _gelu(x):
    return 0.5 * x * (1.0 + lax.erf(x * (1.0 / math.sqrt(2.0))))


def _peer_expert_body(xt_ref, u_ref, v_ref, rows_ref, r2_ref, e2_ref, o_ref, act_ref, w_ref, *, nkeys, tm):
    e = pl.program_id(1)
    ni = PEER_KEYS_PER_STEP

    @pl.when(e == 0)
    def _():
        o_ref[...] = jnp.zeros_like(o_ref)
        act_ref[...] = jnp.zeros_like(act_ref)

    pack = 2 * SUBLANES
    half_keys = nkeys // 2

    def packed_rows(word_row):
        tile = pltpu.bitcast(jnp.broadcast_to(word_row, (SUBLANES, LANES)), BF16)
        return jnp.tile(tile, (nkeys // pack, 1))

    def gate_block(ii, sub):
        rows = slice(ii * nkeys, (ii + 1) * nkeys)
        cols = slice(sub * LANES, (sub + 1) * LANES)
        gate = jnp.zeros((nkeys, LANES), BF16)
        for h in range(PEER_HEADS):
            cnt = packed_rows(rows_ref[h, ii:ii + 1, cols])
            e1 = packed_rows(rows_ref[h, ni + ii:ni + ii + 1, cols])
            keys = slice(h * half_keys, (h + 1) * half_keys)
            r2 = pltpu.bitcast(r2_ref[keys, cols], BF16)
            e2 = pltpu.bitcast(e2_ref[keys, cols], BF16)
            gate = gate + jnp.where(r2 < cnt, e2, jnp.zeros((), BF16)) * e1
        w_ref[rows, cols] = (gate.astype(F32) * _gelu(act_ref[rows, cols])).astype(BF16)

    for ii in range(ni):
        for sub in range(tm // LANES):
            gate_block(ii, sub)

    act_ref[...] = jnp.dot(u_ref[...], xt_ref[...], preferred_element_type=F32)
    o_ref[...] += lax.dot_general(v_ref[...], w_ref[...], (((0,), (0,)), ((), ())),
                                  preferred_element_type=F32)


def peer_experts(xt, u, v, rows, r2, e2):
    d, t = xt.shape
    n_exp = u.shape[0]
    nkeys = 2 * r2.shape[0] // PEER_HEADS
    ni = PEER_KEYS_PER_STEP
    tm = _pick(t, 512)
    eb = ni * nkeys
    n_e = n_exp // eb
    once = dict(pipeline_mode=pl.Buffered(1))
    cur = lambda e: jnp.minimum(e, n_e - 1)
    prev = lambda e: jnp.maximum(e - 1, 0)
    route_spec = lambda: pl.BlockSpec((PEER_HEADS * nkeys // 2, tm), lambda i, e: (0, i), **once)
    return pl.pallas_call(
        functools.partial(_peer_expert_body, nkeys=nkeys, tm=tm),
        out_shape=jax.ShapeDtypeStruct((d, t), F32),
        grid=(t // tm, n_e + 1),
        in_specs=[pl.BlockSpec((d, tm), lambda i, e: (0, i), **once),
                  pl.BlockSpec((eb, d), lambda i, e: (cur(e), 0)),
                  pl.BlockSpec((eb, d), lambda i, e: (prev(e), 0)),
                  pl.BlockSpec((PEER_HEADS, None, 2 * ni, tm), lambda i, e: (0, prev(e), 0, i)),
                  route_spec(), route_spec()],
        out_specs=pl.BlockSpec((d, tm), lambda i, e: (0, i)),
        scratch_shapes=[pltpu.VMEM((eb, tm), F32), pltpu.VMEM((eb, tm), BF16)],
        compiler_params=_params("parallel", "arbitrary"),
        name="peer_experts",
    )(xt, u, v, rows, r2, e2)


def _peer_ffn_t(xt, w_q, subkeys, u, v):
    qt = matmul(w_q.T.astype(BF16), xt, BF16, name="peer_query")
    rows, r2, e2 = peer_route(qt, subkeys.astype(BF16))
    return peer_experts(xt, u.astype(BF16), v.astype(BF16), rows, r2, e2)


def kernel(x, p, ln_mix, ssd_w_in, ssd_conv_w, ssd_conv_b, ssd_dt_bias, ssd_a_log, ssd_d_skip, ssd_norm_w, ssd_w_out, fox_w_in, fox_b_f, fox_w_out, ln_ffn, peer_w_q, peer_subkeys, peer_u, peer_v, ln_ple, ple_w_gate, ple_w_proj, final_norm):
    bsz, seq, d = x.shape
    depth = p.shape[0]
    t = bsz * seq
    h = x.reshape(t, d)
    for i in range(depth):
        j = i // 2
        if i % 2 == 0:
            (hn,) = rmsnorm(h, ln_mix[i])
            d_inner = ssd_w_out.shape[1]
            conv_dim = ssd_conv_w.shape[2]
            w_in = ssd_w_in[j]
            zxbc = matmul(hn, w_in[:, :d_inner + conv_dim].astype(BF16), BF16, name="ssd_in_proj")
            dt_raw = matmul(hn, w_in[:, d_inner + conv_dim:].astype(BF16), F32, name="ssd_dt_proj")
            y = ssd_core(zxbc, dt_raw, ssd_conv_w[j], ssd_conv_b[j], ssd_dt_bias[j], ssd_a_log[j],
                         ssd_d_skip[j], ssd_norm_w[j], bsz, seq)
            h = matmul(y, ssd_w_out[j].astype(BF16), F32, epilogue=_epi_residual, extra=(h,),
                       name="ssd_out_proj")
        else:
            hn, hnt = rmsnorm(h, ln_mix[i], emit_t=True)
            heads = fox_b_f.shape[1]
            w_in = fox_w_in[j]
            qscale = FOX_HEAD_DIM ** -0.5 * LOG2E
            q = matmul(hn, (w_in[:, :d] * qscale).astype(BF16), BF16, name="fox_q_proj")
            kt = matmul(w_in[:, d:2 * d].T.astype(BF16), hnt, BF16, name="fox_k_proj")
            v = matmul(hn, w_in[:, 2 * d:3 * d].astype(BF16), BF16, name="fox_v_proj")
            w_f = jnp.pad(w_in[:, 3 * d:], ((0, 0), (0, LANES - heads))).astype(BF16)
            f_raw = matmul(hn, w_f, F32, name="fox_gate_proj")
            b_f = jnp.pad(fox_b_f[j].astype(F32), (0, LANES - heads)).reshape(1, LANES)
            cum = fox_cum(f_raw, b_f, bsz, seq)
            ck_t = cum.reshape(bsz, seq, LANES)[:, :, :heads].transpose(0, 2, 1).reshape(bsz, heads, 1, seq)
            o = fox_attention(q, kt, v, ck_t, bsz, seq, heads)
            h = matmul(o, fox_w_out[j].astype(BF16), F32, epilogue=_epi_residual, extra=(h,),
                       name="fox_out_proj")
        (hnt,) = rmsnorm(h, ln_ffn[i], emit_n=False, emit_t=True)
        yt = _peer_ffn_t(hnt, peer_w_q[i], peer_subkeys[i], peer_u[i], peer_v[i])
        h, hn = rmsnorm(h, ln_ple[i], add=yt, add_transposed=True)
        p_i = p[i].reshape(t, -1).astype(BF16)
        w_proj = ple_w_proj[i].astype(BF16)
        pdim = p_i.shape[1]
        h = matmul(hn, ple_w_gate[i].astype(BF16), F32, epilogue=_epi_ple, extra=(h, p_i, w_proj),
                   extra_specs=lambda tm, tn: [pl.BlockSpec((tm, tn), lambda a, b, c: (a, b)),
                                               pl.BlockSpec((tm, pdim), lambda a, b, c: (a, 0)),
                                               pl.BlockSpec((pdim, tn), lambda a, b, c: (0, b))],
                   name="ple")
    (out,) = rmsnorm(h, final_norm, out_dtype=F32)
    return out.reshape(bsz, seq, d)
```

```python
import functools
import math

import jax
import jax.numpy as jnp
from jax import lax
from jax.experimental import pallas as pl
from jax.experimental.pallas import tpu as pltpu

F32 = jnp.float32
BF16 = jnp.bfloat16
NEG_INF = float("-inf")
LOG2E = math.log2(math.e)

NORM_EPS = 1e-6
SSD_GROUPS = 8
SSD_HEAD_DIM = 64
SSD_CHUNK = 128
SSD_CONV = 4
FOX_HEAD_DIM = 128
PEER_HEADS = 8
PEER_TOPK = 16
PEER_KEYS_PER_STEP = 4

V7X_VMEM_BYTES = 64 * 1024 * 1024
VMEM_LIMIT = V7X_VMEM_BYTES - 8 * 1024 * 1024
LANES = 128
SUBLANES = 8


def _params(*sem):
    return pltpu.CompilerParams(dimension_semantics=sem, vmem_limit_bytes=VMEM_LIMIT)


def _pick(n, pref):
    t = min(n, pref)
    while n % t:
        t //= 2
    return t


def _norm_rows(x, w_ref):
    return x * lax.rsqrt(jnp.mean(x * x, axis=-1, keepdims=True) + NORM_EPS) * w_ref[...]


def _rmsnorm_body(*refs, has_add, emit_n, emit_t, add_transposed):
    refs = list(refs)
    x = refs.pop(0)[...].astype(F32)
    if has_add:
        y = refs.pop(0)[...]
        x = x + (y.T if add_transposed else y)
    w_ref = refs.pop(0)
    if has_add:
        refs.pop(0)[...] = x
    n = _norm_rows(x, w_ref)
    if emit_n:
        o = refs.pop(0)
        o[...] = n.astype(o.dtype)
    if emit_t:
        o = refs.pop(0)
        o[...] = n.T.astype(o.dtype)


def rmsnorm(x, w, out_dtype=BF16, add=None, add_transposed=False, emit_n=True, emit_t=False):
    t, d = x.shape
    tm = _pick(t, 256)
    row = pl.BlockSpec((tm, d), lambda i: (i, 0))
    col = pl.BlockSpec((d, tm), lambda i: (0, i))
    in_specs, args = [row], [x]
    if add is not None:
        in_specs.append(col if add_transposed else row)
        args.append(add)
    in_specs.append(pl.BlockSpec((1, d), lambda i: (0, 0)))
    args.append(w.reshape(1, d).astype(F32))
    out_shape, out_specs = [], []
    if add is not None:
        out_shape.append(jax.ShapeDtypeStruct((t, d), F32))
        out_specs.append(row)
    if emit_n:
        out_shape.append(jax.ShapeDtypeStruct((t, d), out_dtype))
        out_specs.append(row)
    if emit_t:
        out_shape.append(jax.ShapeDtypeStruct((d, t), out_dtype))
        out_specs.append(col)
    return pl.pallas_call(
        functools.partial(_rmsnorm_body, has_add=add is not None, emit_n=emit_n, emit_t=emit_t,
                          add_transposed=add_transposed),
        out_shape=tuple(out_shape),
        grid=(t // tm,),
        in_specs=in_specs,
        out_specs=tuple(out_specs),
        compiler_params=_params("parallel"),
        name="rmsnorm",
    )(*args)


def _matmul_body(*refs, nk, epilogue, n_extra):
    a_ref, b_ref = refs[0], refs[1]
    extra = refs[2:2 + n_extra]
    o_ref = refs[2 + n_extra]
    if nk == 1:
        acc = jnp.dot(a_ref[...], b_ref[...], preferred_element_type=F32)
        o_ref[...] = epilogue(acc, *extra).astype(o_ref.dtype)
        return
    acc_ref = refs[3 + n_extra]
    k = pl.program_id(2)

    @pl.when(k == 0)
    def _():
        acc_ref[...] = jnp.dot(a_ref[...], b_ref[...], preferred_element_type=F32)

    @pl.when(jnp.logical_and(k > 0, k < nk - 1))
    def _():
        acc_ref[...] += jnp.dot(a_ref[...], b_ref[...], preferred_element_type=F32)

    @pl.when(k == nk - 1)
    def _():
        acc = acc_ref[...] + jnp.dot(a_ref[...], b_ref[...], preferred_element_type=F32)
        o_ref[...] = epilogue(acc, *extra).astype(o_ref.dtype)


def _epi_plain(acc):
    return acc


def _epi_residual(acc, res_ref):
    return res_ref[...] + acc


def _epi_ple(acc, h_ref, p_ref, wp_ref):
    proj = jnp.dot(p_ref[...], wp_ref[...], preferred_element_type=F32)
    return h_ref[...] + proj * jax.nn.sigmoid(acc)


def matmul(a, b, out_dtype, epilogue=_epi_plain, extra=(), extra_specs=None,
           tm=1024, tn=1024, tk=2048, name="matmul"):
    m, kd = a.shape
    _, n = b.shape
    tm, tn, tk = _pick(m, tm), _pick(n, tn), _pick(kd, tk)
    nk = kd // tk
    if extra_specs is None:
        extra_specs = [pl.BlockSpec((tm, tn), lambda i, j, k: (i, j)) for _ in extra]
    else:
        extra_specs = extra_specs(tm, tn)
    return pl.pallas_call(
        functools.partial(_matmul_body, nk=nk, epilogue=epilogue, n_extra=len(extra)),
        out_shape=jax.ShapeDtypeStruct((m, n), out_dtype),
        grid=(m // tm, n // tn, nk),
        in_specs=[pl.BlockSpec((tm, tk), lambda i, j, k: (i, k)),
                  pl.BlockSpec((tk, tn), lambda i, j, k: (k, j))] + list(extra_specs),
        out_specs=pl.BlockSpec((tm, tn), lambda i, j, k: (i, j)),
        scratch_shapes=[pltpu.VMEM((tm, tn), F32)] if nk > 1 else [],
        compiler_params=_params("parallel", "parallel", "arbitrary"),
        name=name,
    )(a, b, *extra)


def _silu(x):
    return x * jax.nn.sigmoid(x)


def _ssd_body(z_ref, x_ref, b_ref, c_ref, dt_ref, dtt_ref, bias_ref, biast_ref, alog_ref, alogt_ref,
              wx_ref, wb_ref, wc_ref, bx_ref, bb_ref, bc_ref, dskip_ref, normw_ref, expand_ref,
              o_ref, state_ref, xpad_ref, bpad_ref, cpad_ref, *, q, hg, p):
    c_idx = pl.program_id(2)
    halo = SUBLANES

    @pl.when(c_idx == 0)
    def _():
        state_ref[...] = jnp.zeros_like(state_ref)
        xpad_ref[0:halo, :] = jnp.zeros((halo, xpad_ref.shape[1]), F32)
        bpad_ref[0:halo, :] = jnp.zeros((halo, bpad_ref.shape[1]), F32)
        cpad_ref[0:halo, :] = jnp.zeros((halo, cpad_ref.shape[1]), F32)

    def conv_silu(pad_ref, cur_ref, w_ref, bias_row_ref):
        pad_ref[halo:halo + q, :] = cur_ref[...].astype(F32)
        acc = bias_row_ref[...]
        for k in range(SSD_CONV):
            off = halo - (SSD_CONV - 1) + k
            acc = acc + w_ref[k:k + 1, :] * pad_ref[off:off + q, :]
        pad_ref[0:halo, :] = pad_ref[q:q + halo, :]
        return _silu(acc)

    xs = conv_silu(xpad_ref, x_ref, wx_ref, bx_ref)
    bm = conv_silu(bpad_ref, b_ref, wb_ref, bb_ref)
    cm = conv_silu(cpad_ref, c_ref, wc_ref, bc_ref)

    dt = jax.nn.softplus(dt_ref[...] + bias_ref[...])
    dtt = jax.nn.softplus(dtt_ref[...] + biast_ref[...])
    a = dt * (-jnp.exp(alog_ref[...]))
    at = dtt * (-jnp.exp(alogt_ref[...]))
    row = lax.broadcasted_iota(jnp.int32, (q, q), 0)
    col = lax.broadcasted_iota(jnp.int32, (q, q), 1)
    causal = row >= col
    tri = causal.astype(F32)
    hi = lax.Precision.HIGHEST
    a_cs = jnp.dot(tri, a, precision=hi, preferred_element_type=F32)
    a_cst = jnp.dot(at, (row <= col).astype(F32), precision=hi, preferred_element_type=F32)

    a_last = a_cs[q - 1:q, :]
    small = jnp.concatenate([dt, jnp.exp(a_cs), jnp.exp(a_last - a_cs)], axis=0)
    small_hi = small.astype(BF16)
    small_lo = (small - small_hi.astype(F32)).astype(BF16)
    full = (jnp.dot(small_hi, expand_ref[...], preferred_element_type=F32)
            + jnp.dot(small_lo, expand_ref[...], preferred_element_type=F32))
    dt_full, dec_full, decay_full = full[0:q], full[q:2 * q], full[2 * q:3 * q]

    x_dt = xs * dt_full
    x_dt_b = x_dt.astype(BF16)
    bm_b = bm.astype(BF16)
    cm_b = cm.astype(BF16)

    cb = lax.dot_general(cm_b, bm_b, (((1,), (1,)), ((), ())), preferred_element_type=F32)

    state = state_ref[...]
    y = jnp.dot(cm_b, state.astype(BF16), preferred_element_type=F32) * dec_full

    lane = lax.broadcasted_iota(jnp.int32, (q, 2 * p), 1)
    y_parts = []
    for pair in range(hg // 2):
        ms = []
        for h in (2 * pair, 2 * pair + 1):
            seg = a_cs[:, h:h + 1] - a_cst[h:h + 1, :]
            ms.append((cb * jnp.exp(jnp.where(causal, seg, NEG_INF))).astype(BF16))
        lhs = jnp.concatenate(ms, axis=1)
        xp = x_dt_b[:, 2 * pair * p:(2 * pair + 2) * p]
        zero = jnp.zeros_like(xp)
        rhs = jnp.concatenate([jnp.where(lane < p, xp, zero), jnp.where(lane >= p, xp, zero)], axis=0)
        y_parts.append(jnp.dot(lhs, rhs, preferred_element_type=F32))
    y = y + jnp.concatenate(y_parts, axis=1)

    xd = (x_dt * decay_full).astype(BF16)
    upd = lax.dot_general(bm_b, xd, (((0,), (0,)), ((), ())), preferred_element_type=F32)
    state_ref[...] = state * dec_full[q - 1:q, :] + upd

    y = y + dskip_ref[...] * xs
    g = y * _silu(z_ref[...].astype(F32))
    g = g * lax.rsqrt(jnp.mean(g * g, axis=-1, keepdims=True) + NORM_EPS)
    o_ref[...] = (g * normw_ref[...]).astype(o_ref.dtype)


def ssd_core(zxbc, dt_raw, conv_w, conv_b, dt_bias, a_log, d_skip, norm_w, bsz, seq):
    t = bsz * seq
    heads = dt_bias.shape[0]
    g = SSD_GROUPS
    hg = heads // g
    p = SSD_HEAD_DIM
    gw = hg * p
    d_inner = heads * p
    n = (zxbc.shape[1] - 2 * d_inner) // (2 * g)
    q = SSD_CHUNK
    nc = seq // q
    assert gw % LANES == 0 and n % LANES == 0 and hg % 2 == 0 and 2 * p == LANES

    dtg = dt_raw.reshape(bsz, seq, g, hg).transpose(0, 2, 1, 3)
    dtgt = dtg.transpose(0, 1, 3, 2)
    bias = dt_bias.reshape(g, 1, hg).astype(F32)
    biast = dt_bias.reshape(g, hg, 1).astype(F32)
    alog = a_log.reshape(g, 1, hg).astype(F32)
    alogt = a_log.reshape(g, hg, 1).astype(F32)
    conv_w = conv_w.astype(F32)
    conv_b = conv_b.reshape(1, -1).astype(F32)
    dskip_full = jnp.repeat(d_skip.astype(F32), p).reshape(1, d_inner)
    normw = norm_w.reshape(1, d_inner).astype(F32)
    expand = (jnp.arange(gw)[None, :] // p == jnp.arange(hg)[:, None]).astype(BF16)

    xblk, bblk, cblk = d_inner // gw, 2 * d_inner // n, (2 * d_inner + g * n) // n
    wx_blk, wb_blk, wc_blk = 0, d_inner // n, (d_inner + g * n) // n
    rows = lambda b, gi, c: b * nc + c
    in_specs = [
        pl.BlockSpec((q, gw), lambda b, gi, c: (rows(b, gi, c), gi)),
        pl.BlockSpec((q, gw), lambda b, gi, c: (rows(b, gi, c), xblk + gi)),
        pl.BlockSpec((q, n), lambda b, gi, c: (rows(b, gi, c), bblk + gi)),
        pl.BlockSpec((q, n), lambda b, gi, c: (rows(b, gi, c), cblk + gi)),
        pl.BlockSpec((None, None, q, hg), lambda b, gi, c: (b, gi, c, 0)),
        pl.BlockSpec((None, None, hg, q), lambda b, gi, c: (b, gi, 0, c)),
        pl.BlockSpec((None, 1, hg), lambda b, gi, c: (gi, 0, 0)),
        pl.BlockSpec((None, hg, 1), lambda b, gi, c: (gi, 0, 0)),
        pl.BlockSpec((None, 1, hg), lambda b, gi, c: (gi, 0, 0)),
        pl.BlockSpec((None, hg, 1), lambda b, gi, c: (gi, 0, 0)),
        pl.BlockSpec((SSD_CONV, gw), lambda b, gi, c: (0, wx_blk + gi)),
        pl.BlockSpec((SSD_CONV, n), lambda b, gi, c: (0, wb_blk + gi)),
        pl.BlockSpec((SSD_CONV, n), lambda b, gi, c: (0, wc_blk + gi)),
        pl.BlockSpec((1, gw), lambda b, gi, c: (0, wx_blk + gi)),
        pl.BlockSpec((1, n), lambda b, gi, c: (0, wb_blk + gi)),
        pl.BlockSpec((1, n), lambda b, gi, c: (0, wc_blk + gi)),
        pl.BlockSpec((1, gw), lambda b, gi, c: (0, gi)),
        pl.BlockSpec((1, gw), lambda b, gi, c: (0, gi)),
        pl.BlockSpec((hg, gw), lambda b, gi, c: (0, 0)),
    ]
    return pl.pallas_call(
        functools.partial(_ssd_body, q=q, hg=hg, p=p),
        out_shape=jax.ShapeDtypeStruct((t, d_inner), BF16),
        grid=(bsz, g, nc),
        in_specs=in_specs,
        out_specs=pl.BlockSpec((q, gw), lambda b, gi, c: (rows(b, gi, c), gi)),
        scratch_shapes=[pltpu.VMEM((n, gw), F32),
                        pltpu.VMEM((q + SUBLANES, gw), F32),
                        pltpu.VMEM((q + SUBLANES, n), F32),
                        pltpu.VMEM((q + SUBLANES, n), F32)],
        compiler_params=_params("parallel", "parallel", "arbitrary"),
        name="ssd_core",
    )(zxbc, zxbc, zxbc, zxbc, dtg, dtgt, bias, biast, alog, alogt,
      conv_w, conv_w, conv_w, conv_b, conv_b, conv_b, dskip_full, normw, expand)


def _fox_cum_body(f_ref, bf_ref, o_ref, carry_ref, *, tl):
    @pl.when(pl.program_id(1) == 0)
    def _():
        carry_ref[...] = jnp.zeros_like(carry_ref)

    logf = jax.nn.log_sigmoid(f_ref[...] + bf_ref[...])
    row = lax.broadcasted_iota(jnp.int32, (tl, tl), 0)
    col = lax.broadcasted_iota(jnp.int32, (tl, tl), 1)
    tri = (row >= col).astype(F32)
    cum = jnp.dot(tri, logf, precision=lax.Precision.HIGHEST, preferred_element_type=F32) + carry_ref[...]
    o_ref[...] = cum * LOG2E
    carry_ref[...] = cum[tl - 1:tl, :]


def fox_cum(f_raw, b_f, bsz, seq):
    t, w = f_raw.shape
    tl = _pick(seq, 256)
    nl = seq // tl
    return pl.pallas_call(
        functools.partial(_fox_cum_body, tl=tl),
        out_shape=jax.ShapeDtypeStruct((t, w), F32),
        grid=(bsz, nl),
        in_specs=[pl.BlockSpec((tl, w), lambda b, i: (b * nl + i, 0)),
                  pl.BlockSpec((1, w), lambda b, i: (0, 0))],
        out_specs=pl.BlockSpec((tl, w), lambda b, i: (b * nl + i, 0)),
        scratch_shapes=[pltpu.VMEM((1, w), F32)],
        compiler_params=_params("parallel", "arbitrary"),
        name="fox_cum",
    )(f_raw, b_f)


def _fox_attn_body(iq_tab, ik_tab, q_ref, kt_ref, v_ref, ck_ref, o_ref, m_ref, l_ref, acc_ref, *, tb, rb):
    step = pl.program_id(2)
    iq = iq_tab[step]
    ik = ik_tab[step]

    @pl.when(ik == 0)
    def _():
        m_ref[...] = jnp.full_like(m_ref, NEG_INF)
        l_ref[...] = jnp.zeros_like(l_ref)
        acc_ref[...] = jnp.zeros_like(acc_ref)

    def run(masked):
        def width(r0):
            return r0 + rb if masked else tb

        def logits(r0):
            return jnp.dot(q_ref[r0:r0 + rb, :], kt_ref[:, 0:width(r0)], preferred_element_type=F32)

        starts = list(range(0, tb, rb))
        ahead = 2
        pending = [logits(r0) for r0 in starts[:ahead]]
        for idx, r0 in enumerate(starts):
            rows = slice(r0, r0 + rb)
            wd = width(r0)
            if idx + ahead < len(starts):
                pending.append(logits(starts[idx + ahead]))
            s = pending.pop(0) - ck_ref[:, 0:wd]
            if masked:
                qpos = r0 + lax.broadcasted_iota(jnp.int32, (rb, wd), 0)
                kpos = lax.broadcasted_iota(jnp.int32, (rb, wd), 1)
                s = jnp.where(kpos <= qpos, s, NEG_INF)
            m_prev = m_ref[rows, :]
            m_new = jnp.maximum(m_prev, jnp.max(s, axis=1, keepdims=True))
            alpha = jnp.exp2(m_prev - m_new)
            pr = jnp.exp2(s - jnp.tile(m_new, (1, wd // LANES)))
            l_ref[rows, :] = alpha * l_ref[rows, :] + jnp.sum(pr, axis=1, keepdims=True)
            acc_ref[rows, :] = alpha * acc_ref[rows, :] + jnp.dot(pr.astype(BF16), v_ref[0:wd, :],
                                                                  preferred_element_type=F32)
            m_ref[rows, :] = m_new

    @pl.when(ik < iq)
    def _():
        run(False)

    @pl.when(ik == iq)
    def _():
        run(True)
        o_ref[...] = (acc_ref[...] / l_ref[...]).astype(o_ref.dtype)


def fox_attention(q, kt, v, ck_t, bsz, seq, heads):
    t, d = q.shape
    hd = FOX_HEAD_DIM
    assert hd == LANES
    tb = _pick(seq, 1024)
    nb = seq // tb
    pairs = [(iq, ik) for iq in range(nb) for ik in range(iq + 1)]
    iq_tab = jnp.asarray([pr[0] for pr in pairs], jnp.int32)
    ik_tab = jnp.asarray([pr[1] for pr in pairs], jnp.int32)
    grid_spec = pltpu.PrefetchScalarGridSpec(
        num_scalar_prefetch=2,
        grid=(bsz, heads, len(pairs)),
        in_specs=[
            pl.BlockSpec((tb, hd), lambda b, h, s, iqt, ikt: (b * nb + iqt[s], h)),
            pl.BlockSpec((hd, tb), lambda b, h, s, iqt, ikt: (h, b * nb + ikt[s])),
            pl.BlockSpec((tb, hd), lambda b, h, s, iqt, ikt: (b * nb + ikt[s], h)),
            pl.BlockSpec((None, None, 1, tb), lambda b, h, s, iqt, ikt: (b, h, 0, ikt[s])),
        ],
        out_specs=pl.BlockSpec((tb, hd), lambda b, h, s, iqt, ikt: (b * nb + iqt[s], h)),
        scratch_shapes=[pltpu.VMEM((tb, LANES), F32), pltpu.VMEM((tb, LANES), F32), pltpu.VMEM((tb, hd), F32)],
    )
    return pl.pallas_call(
        functools.partial(_fox_attn_body, tb=tb, rb=_pick(tb, 128)),
        out_shape=jax.ShapeDtypeStruct((t, d), BF16),
        grid_spec=grid_spec,
        compiler_params=_params("parallel", "parallel", "arbitrary"),
        name="fox_attention",
    )(iq_tab, ik_tab, q, kt, v, ck_t)


def _top_rows(blocks, k, with_rank=False):
    out = []
    ranks = [jnp.full(w.shape, float(k), F32) for w in blocks] if with_rank else None
    for r in range(k):
        m = functools.reduce(jnp.maximum, blocks)
        m = jnp.max(m, axis=0, keepdims=True)
        out.append(m)
        hit = [w == m for w in blocks]
        if with_rank:
            ranks = [jnp.where(e, float(r), rk) for e, rk in zip(hit, ranks)]
        blocks = [jnp.where(e, NEG_INF, w) for e, w in zip(hit, blocks)]
    return (out, ranks) if with_rank else out


def _bf16_pair_words(x):
    u = pltpu.bitcast(x, jnp.uint32)
    lsb = lax.shift_right_logical(u, jnp.uint32(16)) & jnp.uint32(1)
    r = lax.shift_right_logical(u + jnp.uint32(0x7FFF) + lsb, jnp.uint32(16))
    return r | lax.shift_left(r, jnp.uint32(16))


def _peer_route_body(qt_ref, keys_ref, rows_ref, r2_ref, e2_ref, *, nkeys, kd):
    k = PEER_TOPK
    ni = PEER_KEYS_PER_STEP
    rb = SUBLANES
    for h in range(PEER_HEADS):
        base = h * 2 * kd
        s1 = jnp.dot(keys_ref[0], qt_ref[base:base + kd, :], preferred_element_type=F32)
        s2 = jnp.dot(keys_ref[1], qt_ref[base + kd:base + 2 * kd, :], preferred_element_type=F32)
        v1 = jnp.concatenate(_top_rows([s1[r:r + rb] for r in range(0, nkeys, rb)], k), axis=0)
        v2_rows, rank2 = _top_rows([s2[r:r + rb] for r in range(0, nkeys, rb)], k, with_rank=True)
        v2 = jnp.concatenate(v2_rows, axis=0)
        half = k // 2
        cands = [v1[0:1] + v2[0:half], v1[0:1] + v2[half:k]]
        cands += [v1[a:a + 1] + v2[0:half] for a in range(1, half)]
        cands += [v1[half:k] + v2[0:1]]
        sc = _top_rows(cands, k)
        m, tau = sc[0], sc[k - 1]
        z = jnp.zeros_like(m)
        for r in sc:
            z = z + jnp.exp(r - m)
        e1 = jnp.exp(s1 - v1[0:1]) / z
        cnt = jnp.zeros_like(s1)
        for b in range(k):
            cnt = cnt + jnp.where(s1 + v2_rows[b] >= tau, 1.0, 0.0)
        cnt_w, e1_w = _bf16_pair_words(cnt), _bf16_pair_words(e1)
        for blk in range(nkeys // ni):
            rows_ref[h, blk, 0:ni, :] = cnt_w[blk * ni:(blk + 1) * ni]
            rows_ref[h, blk, ni:2 * ni, :] = e1_w[blk * ni:(blk + 1) * ni]
        half_keys = nkeys // 2
        r2 = jnp.concatenate(rank2, axis=0).astype(BF16)
        e2 = jnp.exp(s2 - v2[0:1]).astype(BF16)
        r2_ref[h * half_keys:(h + 1) * half_keys, :] = pltpu.bitcast(r2, jnp.uint32)
        e2_ref[h * half_keys:(h + 1) * half_keys, :] = pltpu.bitcast(e2, jnp.uint32)


def peer_route(qt, keys):
    rows, t = qt.shape
    _, nkeys, kd = keys.shape
    ni = PEER_KEYS_PER_STEP
    tm = _pick(t, 256)
    big = jax.ShapeDtypeStruct((PEER_HEADS * nkeys // 2, t), jnp.uint32)
    big_spec = pl.BlockSpec((PEER_HEADS * nkeys // 2, tm), lambda i: (0, i))
    return pl.pallas_call(
        functools.partial(_peer_route_body, nkeys=nkeys, kd=kd),
        out_shape=(jax.ShapeDtypeStruct((PEER_HEADS, nkeys // ni, 2 * ni, t), jnp.uint32), big, big),
        grid=(t // tm,),
        in_specs=[pl.BlockSpec((rows, tm), lambda i: (0, i)),
                  pl.BlockSpec((2, nkeys, kd), lambda i: (0, 0, 0))],
        out_specs=(pl.BlockSpec((PEER_HEADS, nkeys // ni, 2 * ni, tm), lambda i: (0, 0, 0, i)),
                   big_spec, big_spec),
        compiler_params=_params("parallel"),
        name="peer_route",
    )(qt, keys)


def _gelu(x):
    return 0.5 * x * (1.0 + lax.erf(x * (1.0 / math.sqrt(2.0))))


def _peer_expert_body(xt_ref, u_ref, vt_ref, rows_ref, r2_ref, e2_ref, o_ref, act_ref, w_ref, *, nkeys, tm):
    e = pl.program_id(1)
    ni = PEER_KEYS_PER_STEP

    @pl.when(e == 0)
    def _():
        o_ref[...] = jnp.zeros_like(o_ref)
        act_ref[...] = jnp.zeros_like(act_ref)

    pack = 2 * SUBLANES
    half_keys = nkeys // 2

    def packed_rows(word_row):
        tile = pltpu.bitcast(jnp.broadcast_to(word_row, (SUBLANES, LANES)), BF16)
        return jnp.tile(tile, (nkeys // pack, 1))

    def gate_block(ii, sub):
        rows = slice(ii * nkeys, (ii + 1) * nkeys)
        cols = slice(sub * LANES, (sub + 1) * LANES)
        gate = jnp.zeros((nkeys, LANES), BF16)
        for h in range(PEER_HEADS):
            cnt = packed_rows(rows_ref[h, ii:ii + 1, cols])
            e1 = packed_rows(rows_ref[h, ni + ii:ni + ii + 1, cols])
            keys = slice(h * half_keys, (h + 1) * half_keys)
            r2 = pltpu.bitcast(r2_ref[keys, cols], BF16)
            e2 = pltpu.bitcast(e2_ref[keys, cols], BF16)
            gate = gate + jnp.where(r2 < cnt, e2, jnp.zeros((), BF16)) * e1
        w_ref[rows, cols] = (gate.astype(F32) * _gelu(act_ref[rows, cols])).astype(BF16)

    for ii in range(ni):
        for sub in range(tm // LANES):
            gate_block(ii, sub)

    act_ref[...] = jnp.dot(u_ref[...], xt_ref[...], preferred_element_type=F32)
    o_ref[...] += jnp.dot(vt_ref[...], w_ref[...], preferred_element_type=F32)


def peer_experts(xt, u, vt_blocks, rows, r2, e2):
    d, t = xt.shape
    n_exp = u.shape[0]
    nkeys = 2 * r2.shape[0] // PEER_HEADS
    ni = PEER_KEYS_PER_STEP
    tm = _pick(t, 512)
    n_e, _, eb = vt_blocks.shape
    assert eb == ni * nkeys and n_e * eb == n_exp
    once = dict(pipeline_mode=pl.Buffered(1))
    cur = lambda e: jnp.minimum(e, n_e - 1)
    prev = lambda e: jnp.maximum(e - 1, 0)
    route_spec = lambda: pl.BlockSpec((PEER_HEADS * nkeys // 2, tm), lambda i, e: (0, i), **once)
    return pl.pallas_call(
        functools.partial(_peer_expert_body, nkeys=nkeys, tm=tm),
        out_shape=jax.ShapeDtypeStruct((d, t), F32),
        grid=(t // tm, n_e + 1),
        in_specs=[pl.BlockSpec((d, tm), lambda i, e: (0, i), **once),
                  pl.BlockSpec((eb, d), lambda i, e: (cur(e), 0)),
                  pl.BlockSpec((None, d, eb), lambda i, e: (prev(e), 0, 0)),
                  pl.BlockSpec((PEER_HEADS, None, 2 * ni, tm), lambda i, e: (0, prev(e), 0, i)),
                  route_spec(), route_spec()],
        out_specs=pl.BlockSpec((d, tm), lambda i, e: (0, i)),
        scratch_shapes=[pltpu.VMEM((eb, tm), F32), pltpu.VMEM((eb, tm), BF16)],
        compiler_params=_params("parallel", "arbitrary"),
        name="peer_experts",
    )(xt, u, vt_blocks, rows, r2, e2)


def _peer_ffn_t(xt, w_q, subkeys, u, v):
    qt = matmul(w_q.T.astype(BF16), xt, BF16, name="peer_query")
    rows, r2, e2 = peer_route(qt, subkeys.astype(BF16))
    n_exp, d = v.shape
    eb = PEER_KEYS_PER_STEP * subkeys.shape[1]
    vt_blocks = v.astype(BF16).reshape(n_exp // eb, eb, d).transpose(0, 2, 1)
    return peer_experts(xt, u.astype(BF16), vt_blocks, rows, r2, e2)


def kernel(x, p, ln_mix, ssd_w_in, ssd_conv_w, ssd_conv_b, ssd_dt_bias, ssd_a_log, ssd_d_skip, ssd_norm_w, ssd_w_out, fox_w_in, fox_b_f, fox_w_out, ln_ffn, peer_w_q, peer_subkeys, peer_u, peer_v, ln_ple, ple_w_gate, ple_w_proj, final_norm):
    bsz, seq, d = x.shape
    depth = p.shape[0]
    t = bsz * seq
    h = x.reshape(t, d)
    for i in range(depth):
        j = i // 2
        if i % 2 == 0:
            (hn,) = rmsnorm(h, ln_mix[i])
            d_inner = ssd_w_out.shape[1]
            conv_dim = ssd_conv_w.shape[2]
            w_in = ssd_w_in[j]
            zxbc = matmul(hn, w_in[:, :d_inner + conv_dim].astype(BF16), BF16, name="ssd_in_proj")
            dt_raw = matmul(hn, w_in[:, d_inner + conv_dim:].astype(BF16), F32, name="ssd_dt_proj")
            y = ssd_core(zxbc, dt_raw, ssd_conv_w[j], ssd_conv_b[j], ssd_dt_bias[j], ssd_a_log[j],
                         ssd_d_skip[j], ssd_norm_w[j], bsz, seq)
            h = matmul(y, ssd_w_out[j].astype(BF16), F32, epilogue=_epi_residual, extra=(h,),
                       name="ssd_out_proj")
        else:
            hn, hnt = rmsnorm(h, ln_mix[i], emit_t=True)
            heads = fox_b_f.shape[1]
            w_in = fox_w_in[j]
            qscale = FOX_HEAD_DIM ** -0.5 * LOG2E
            q = matmul(hn, (w_in[:, :d] * qscale).astype(BF16), BF16, name="fox_q_proj")
            kt = matmul(w_in[:, d:2 * d].T.astype(BF16), hnt, BF16, name="fox_k_proj")
            v = matmul(hn, w_in[:, 2 * d:3 * d].astype(BF16), BF16, name="fox_v_proj")
            w_f = jnp.pad(w_in[:, 3 * d:], ((0, 0), (0, LANES - heads))).astype(BF16)
            f_raw = matmul(hn, w_f, F32, name="fox_gate_proj")
            b_f = jnp.pad(fox_b_f[j].astype(F32), (0, LANES - heads)).reshape(1, LANES)
            cum = fox_cum(f_raw, b_f, bsz, seq)
            ck_t = cum.reshape(bsz, seq, LANES)[:, :, :heads].transpose(0, 2, 1).reshape(bsz, heads, 1, seq)
            o = fox_attention(q, kt, v, ck_t, bsz, seq, heads)
            h = matmul(o, fox_w_out[j].astype(BF16), F32, epilogue=_epi_residual, extra=(h,),
                       name="fox_out_proj")
        (hnt,) = rmsnorm(h, ln_ffn[i], emit_n=False, emit_t=True)
        yt = _peer_ffn_t(hnt, peer_w_q[i], peer_subkeys[i], peer_u[i], peer_v[i])
        h, hn = rmsnorm(h, ln_ple[i], add=yt, add_transposed=True)
        p_i = p[i].reshape(t, -1).astype(BF16)
        w_proj = ple_w_proj[i].astype(BF16)
        pdim = p_i.shape[1]
        h = matmul(hn, ple_w_gate[i].astype(BF16), F32, epilogue=_epi_ple, extra=(h, p_i, w_proj),
                   extra_specs=lambda tm, tn: [pl.BlockSpec((tm, tn), lambda a, b, c: (a, b)),
                                               pl.BlockSpec((tm, pdim), lambda a, b, c: (a, 0)),
                                               pl.BlockSpec((pdim, tn), lambda a, b, c: (0, b))],
                   name="ple")
    (out,) = rmsnorm(h, final_norm, out_dtype=F32)
    return out.reshape(bsz, seq, d)
```

```python
import functools
import math

import jax
import jax.numpy as jnp
from jax import lax
from jax.experimental import pallas as pl
from jax.experimental.pallas import tpu as pltpu

F32 = jnp.float32
BF16 = jnp.bfloat16
NEG_INF = float("-inf")
LOG2E = math.log2(math.e)

NORM_EPS = 1e-6
SSD_GROUPS = 8
SSD_HEAD_DIM = 64
SSD_CHUNK = 128
SSD_CONV = 4
FOX_HEAD_DIM = 128
PEER_HEADS = 8
PEER_TOPK = 16
PEER_KEYS_PER_STEP = 4

V7X_VMEM_BYTES = 64 * 1024 * 1024
VMEM_LIMIT = V7X_VMEM_BYTES - 8 * 1024 * 1024
LANES = 128
SUBLANES = 8


def _params(*sem):
    return pltpu.CompilerParams(dimension_semantics=sem, vmem_limit_bytes=VMEM_LIMIT)


def _pick(n, pref):
    t = min(n, pref)
    while n % t:
        t //= 2
    return t


def _norm_rows(x, w_ref):
    return x * lax.rsqrt(jnp.mean(x * x, axis=-1, keepdims=True) + NORM_EPS) * w_ref[...]


def _rmsnorm_body(*refs, has_add, emit_n, emit_t, add_transposed):
    refs = list(refs)
    x = refs.pop(0)[...].astype(F32)
    if has_add:
        y = refs.pop(0)[...]
        x = x + (y.T if add_transposed else y)
    w_ref = refs.pop(0)
    if has_add:
        refs.pop(0)[...] = x
    n = _norm_rows(x, w_ref)
    if emit_n:
        o = refs.pop(0)
        o[...] = n.astype(o.dtype)
    if emit_t:
        o = refs.pop(0)
        o[...] = n.T.astype(o.dtype)


def rmsnorm(x, w, out_dtype=BF16, add=None, add_transposed=False, emit_n=True, emit_t=False):
    t, d = x.shape
    tm = _pick(t, 256)
    row = pl.BlockSpec((tm, d), lambda i: (i, 0))
    col = pl.BlockSpec((d, tm), lambda i: (0, i))
    in_specs, args = [row], [x]
    if add is not None:
        in_specs.append(col if add_transposed else row)
        args.append(add)
    in_specs.append(pl.BlockSpec((1, d), lambda i: (0, 0)))
    args.append(w.reshape(1, d).astype(F32))
    out_shape, out_specs = [], []
    if add is not None:
        out_shape.append(jax.ShapeDtypeStruct((t, d), F32))
        out_specs.append(row)
    if emit_n:
        out_shape.append(jax.ShapeDtypeStruct((t, d), out_dtype))
        out_specs.append(row)
    if emit_t:
        out_shape.append(jax.ShapeDtypeStruct((d, t), out_dtype))
        out_specs.append(col)
    return pl.pallas_call(
        functools.partial(_rmsnorm_body, has_add=add is not None, emit_n=emit_n, emit_t=emit_t,
                          add_transposed=add_transposed),
        out_shape=tuple(out_shape),
        grid=(t // tm,),
        in_specs=in_specs,
        out_specs=tuple(out_specs),
        compiler_params=_params("parallel"),
        name="rmsnorm",
    )(*args)


def _matmul_body(*refs, nk, epilogue, n_extra):
    a_ref, b_ref = refs[0], refs[1]
    extra = refs[2:2 + n_extra]
    o_ref = refs[2 + n_extra]
    if nk == 1:
        acc = jnp.dot(a_ref[...], b_ref[...], preferred_element_type=F32)
        o_ref[...] = epilogue(acc, *extra).astype(o_ref.dtype)
        return
    acc_ref = refs[3 + n_extra]
    k = pl.program_id(2)

    @pl.when(k == 0)
    def _():
        acc_ref[...] = jnp.dot(a_ref[...], b_ref[...], preferred_element_type=F32)

    @pl.when(jnp.logical_and(k > 0, k < nk - 1))
    def _():
        acc_ref[...] += jnp.dot(a_ref[...], b_ref[...], preferred_element_type=F32)

    @pl.when(k == nk - 1)
    def _():
        acc = acc_ref[...] + jnp.dot(a_ref[...], b_ref[...], preferred_element_type=F32)
        o_ref[...] = epilogue(acc, *extra).astype(o_ref.dtype)


def _epi_plain(acc):
    return acc


def _epi_residual(acc, res_ref):
    return res_ref[...] + acc


def _epi_ple(acc, h_ref, p_ref, wp_ref):
    proj = jnp.dot(p_ref[...], wp_ref[...], preferred_element_type=F32)
    return h_ref[...] + proj * jax.nn.sigmoid(acc)


def matmul(a, b, out_dtype, epilogue=_epi_plain, extra=(), extra_specs=None,
           tm=1024, tn=1024, tk=2048, name="matmul"):
    m, kd = a.shape
    _, n = b.shape
    tm, tn, tk = _pick(m, tm), _pick(n, tn), _pick(kd, tk)
    nk = kd // tk
    if extra_specs is None:
        extra_specs = [pl.BlockSpec((tm, tn), lambda i, j, k: (i, j)) for _ in extra]
    else:
        extra_specs = extra_specs(tm, tn)
    return pl.pallas_call(
        functools.partial(_matmul_body, nk=nk, epilogue=epilogue, n_extra=len(extra)),
        out_shape=jax.ShapeDtypeStruct((m, n), out_dtype),
        grid=(m // tm, n // tn, nk),
        in_specs=[pl.BlockSpec((tm, tk), lambda i, j, k: (i, k)),
                  pl.BlockSpec((tk, tn), lambda i, j, k: (k, j))] + list(extra_specs),
        out_specs=pl.BlockSpec((tm, tn), lambda i, j, k: (i, j)),
        scratch_shapes=[pltpu.VMEM((tm, tn), F32)] if nk > 1 else [],
        compiler_params=_params("parallel", "parallel", "arbitrary"),
        name=name,
    )(a, b, *extra)


def _silu(x):
    return x * jax.nn.sigmoid(x)


def _ssd_body(z_ref, x_ref, b_ref, c_ref, dt_ref, dtt_ref, bias_ref, biast_ref, alog_ref, alogt_ref,
              wx_ref, wb_ref, wc_ref, bx_ref, bb_ref, bc_ref, dskip_ref, normw_ref, expand_ref,
              o_ref, state_ref, xpad_ref, bpad_ref, cpad_ref, *, q, hg, p):
    c_idx = pl.program_id(2)
    halo = SUBLANES

    @pl.when(c_idx == 0)
    def _():
        state_ref[...] = jnp.zeros_like(state_ref)
        xpad_ref[0:halo, :] = jnp.zeros((halo, xpad_ref.shape[1]), F32)
        bpad_ref[0:halo, :] = jnp.zeros((halo, bpad_ref.shape[1]), F32)
        cpad_ref[0:halo, :] = jnp.zeros((halo, cpad_ref.shape[1]), F32)

    def conv_silu(pad_ref, cur_ref, w_ref, bias_row_ref):
        pad_ref[halo:halo + q, :] = cur_ref[...].astype(F32)
        acc = bias_row_ref[...]
        for k in range(SSD_CONV):
            off = halo - (SSD_CONV - 1) + k
            acc = acc + w_ref[k:k + 1, :] * pad_ref[off:off + q, :]
        pad_ref[0:halo, :] = pad_ref[q:q + halo, :]
        return _silu(acc)

    xs = conv_silu(xpad_ref, x_ref, wx_ref, bx_ref)
    bm = conv_silu(bpad_ref, b_ref, wb_ref, bb_ref)
    cm = conv_silu(cpad_ref, c_ref, wc_ref, bc_ref)

    dt = jax.nn.softplus(dt_ref[...] + bias_ref[...])
    dtt = jax.nn.softplus(dtt_ref[...] + biast_ref[...])
    a = dt * (-jnp.exp(alog_ref[...]))
    at = dtt * (-jnp.exp(alogt_ref[...]))
    row = lax.broadcasted_iota(jnp.int32, (q, q), 0)
    col = lax.broadcasted_iota(jnp.int32, (q, q), 1)
    causal = row >= col
    tri = causal.astype(F32)
    hi = lax.Precision.HIGHEST
    a_cs = jnp.dot(tri, a, precision=hi, preferred_element_type=F32)
    a_cst = jnp.dot(at, (row <= col).astype(F32), precision=hi, preferred_element_type=F32)

    a_last = a_cs[q - 1:q, :]
    small = jnp.concatenate([dt, jnp.exp(a_cs), jnp.exp(a_last - a_cs)], axis=0)
    small_hi = small.astype(BF16)
    small_lo = (small - small_hi.astype(F32)).astype(BF16)
    full = (jnp.dot(small_hi, expand_ref[...], preferred_element_type=F32)
            + jnp.dot(small_lo, expand_ref[...], preferred_element_type=F32))
    dt_full, dec_full, decay_full = full[0:q], full[q:2 * q], full[2 * q:3 * q]

    x_dt = xs * dt_full
    x_dt_b = x_dt.astype(BF16)
    bm_b = bm.astype(BF16)
    cm_b = cm.astype(BF16)

    cb = lax.dot_general(cm_b, bm_b, (((1,), (1,)), ((), ())), preferred_element_type=F32)

    state = state_ref[...]
    y = jnp.dot(cm_b, state.astype(BF16), preferred_element_type=F32) * dec_full

    lane = lax.broadcasted_iota(jnp.int32, (q, 2 * p), 1)
    y_parts = []
    for pair in range(hg // 2):
        ms = []
        for h in (2 * pair, 2 * pair + 1):
            seg = a_cs[:, h:h + 1] - a_cst[h:h + 1, :]
            ms.append((cb * jnp.exp(jnp.where(causal, seg, NEG_INF))).astype(BF16))
        lhs = jnp.concatenate(ms, axis=1)
        xp = x_dt_b[:, 2 * pair * p:(2 * pair + 2) * p]
        zero = jnp.zeros_like(xp)
        rhs = jnp.concatenate([jnp.where(lane < p, xp, zero), jnp.where(lane >= p, xp, zero)], axis=0)
        y_parts.append(jnp.dot(lhs, rhs, preferred_element_type=F32))
    y = y + jnp.concatenate(y_parts, axis=1)

    xd = (x_dt * decay_full).astype(BF16)
    upd = lax.dot_general(bm_b, xd, (((0,), (0,)), ((), ())), preferred_element_type=F32)
    state_ref[...] = state * dec_full[q - 1:q, :] + upd

    y = y + dskip_ref[...] * xs
    g = y * _silu(z_ref[...].astype(F32))
    g = g * lax.rsqrt(jnp.mean(g * g, axis=-1, keepdims=True) + NORM_EPS)
    o_ref[...] = (g * normw_ref[...]).astype(o_ref.dtype)


def ssd_core(zxbc, dt_raw, conv_w, conv_b, dt_bias, a_log, d_skip, norm_w, bsz, seq):
    t = bsz * seq
    heads = dt_bias.shape[0]
    g = SSD_GROUPS
    hg = heads // g
    p = SSD_HEAD_DIM
    gw = hg * p
    d_inner = heads * p
    n = (zxbc.shape[1] - 2 * d_inner) // (2 * g)
    q = SSD_CHUNK
    nc = seq // q
    assert gw % LANES == 0 and n % LANES == 0 and hg % 2 == 0 and 2 * p == LANES

    dtg = dt_raw.reshape(bsz, seq, g, hg).transpose(0, 2, 1, 3)
    dtgt = dtg.transpose(0, 1, 3, 2)
    bias = dt_bias.reshape(g, 1, hg).astype(F32)
    biast = dt_bias.reshape(g, hg, 1).astype(F32)
    alog = a_log.reshape(g, 1, hg).astype(F32)
    alogt = a_log.reshape(g, hg, 1).astype(F32)
    conv_w = conv_w.astype(F32)
    conv_b = conv_b.reshape(1, -1).astype(F32)
    dskip_full = jnp.repeat(d_skip.astype(F32), p).reshape(1, d_inner)
    normw = norm_w.reshape(1, d_inner).astype(F32)
    expand = (jnp.arange(gw)[None, :] // p == jnp.arange(hg)[:, None]).astype(BF16)

    xblk, bblk, cblk = d_inner // gw, 2 * d_inner // n, (2 * d_inner + g * n) // n
    wx_blk, wb_blk, wc_blk = 0, d_inner // n, (d_inner + g * n) // n
    rows = lambda b, gi, c: b * nc + c
    in_specs = [
        pl.BlockSpec((q, gw), lambda b, gi, c: (rows(b, gi, c), gi)),
        pl.BlockSpec((q, gw), lambda b, gi, c: (rows(b, gi, c), xblk + gi)),
        pl.BlockSpec((q, n), lambda b, gi, c: (rows(b, gi, c), bblk + gi)),
        pl.BlockSpec((q, n), lambda b, gi, c: (rows(b, gi, c), cblk + gi)),
        pl.BlockSpec((None, None, q, hg), lambda b, gi, c: (b, gi, c, 0)),
        pl.BlockSpec((None, None, hg, q), lambda b, gi, c: (b, gi, 0, c)),
        pl.BlockSpec((None, 1, hg), lambda b, gi, c: (gi, 0, 0)),
        pl.BlockSpec((None, hg, 1), lambda b, gi, c: (gi, 0, 0)),
        pl.BlockSpec((None, 1, hg), lambda b, gi, c: (gi, 0, 0)),
        pl.BlockSpec((None, hg, 1), lambda b, gi, c: (gi, 0, 0)),
        pl.BlockSpec((SSD_CONV, gw), lambda b, gi, c: (0, wx_blk + gi)),
        pl.BlockSpec((SSD_CONV, n), lambda b, gi, c: (0, wb_blk + gi)),
        pl.BlockSpec((SSD_CONV, n), lambda b, gi, c: (0, wc_blk + gi)),
        pl.BlockSpec((1, gw), lambda b, gi, c: (0, wx_blk + gi)),
        pl.BlockSpec((1, n), lambda b, gi, c: (0, wb_blk + gi)),
        pl.BlockSpec((1, n), lambda b, gi, c: (0, wc_blk + gi)),
        pl.BlockSpec((1, gw), lambda b, gi, c: (0, gi)),
        pl.BlockSpec((1, gw), lambda b, gi, c: (0, gi)),
        pl.BlockSpec((hg, gw), lambda b, gi, c: (0, 0)),
    ]
    return pl.pallas_call(
        functools.partial(_ssd_body, q=q, hg=hg, p=p),
        out_shape=jax.ShapeDtypeStruct((t, d_inner), BF16),
        grid=(bsz, g, nc),
        in_specs=in_specs,
        out_specs=pl.BlockSpec((q, gw), lambda b, gi, c: (rows(b, gi, c), gi)),
        scratch_shapes=[pltpu.VMEM((n, gw), F32),
                        pltpu.VMEM((q + SUBLANES, gw), F32),
                        pltpu.VMEM((q + SUBLANES, n), F32),
                        pltpu.VMEM((q + SUBLANES, n), F32)],
        compiler_params=_params("parallel", "parallel", "arbitrary"),
        name="ssd_core",
    )(zxbc, zxbc, zxbc, zxbc, dtg, dtgt, bias, biast, alog, alogt,
      conv_w, conv_w, conv_w, conv_b, conv_b, conv_b, dskip_full, normw, expand)


def _fox_cum_body(f_ref, bf_ref, o_ref, carry_ref, *, tl):
    @pl.when(pl.program_id(1) == 0)
    def _():
        carry_ref[...] = jnp.zeros_like(carry_ref)

    logf = jax.nn.log_sigmoid(f_ref[...] + bf_ref[...])
    row = lax.broadcasted_iota(jnp.int32, (tl, tl), 0)
    col = lax.broadcasted_iota(jnp.int32, (tl, tl), 1)
    tri = (row >= col).astype(F32)
    cum = jnp.dot(tri, logf, precision=lax.Precision.HIGHEST, preferred_element_type=F32) + carry_ref[...]
    o_ref[...] = cum * LOG2E
    carry_ref[...] = cum[tl - 1:tl, :]


def fox_cum(f_raw, b_f, bsz, seq):
    t, w = f_raw.shape
    tl = _pick(seq, 256)
    nl = seq // tl
    return pl.pallas_call(
        functools.partial(_fox_cum_body, tl=tl),
        out_shape=jax.ShapeDtypeStruct((t, w), F32),
        grid=(bsz, nl),
        in_specs=[pl.BlockSpec((tl, w), lambda b, i: (b * nl + i, 0)),
                  pl.BlockSpec((1, w), lambda b, i: (0, 0))],
        out_specs=pl.BlockSpec((tl, w), lambda b, i: (b * nl + i, 0)),
        scratch_shapes=[pltpu.VMEM((1, w), F32)],
        compiler_params=_params("parallel", "arbitrary"),
        name="fox_cum",
    )(f_raw, b_f)


def _fox_attn_body(iq_tab, ik_tab, q_ref, kt_ref, v_ref, ck_ref, o_ref, m_ref, l_ref, acc_ref, *, tb, rb):
    step = pl.program_id(2)
    iq = iq_tab[step]
    ik = ik_tab[step]

    @pl.when(ik == 0)
    def _():
        m_ref[...] = jnp.full_like(m_ref, NEG_INF)
        l_ref[...] = jnp.zeros_like(l_ref)
        acc_ref[...] = jnp.zeros_like(acc_ref)

    def run(masked):
        def width(r0):
            return r0 + rb if masked else tb

        def logits(r0):
            return jnp.dot(q_ref[r0:r0 + rb, :], kt_ref[:, 0:width(r0)], preferred_element_type=F32)

        starts = list(range(0, tb, rb))
        ahead = 2
        pending = [logits(r0) for r0 in starts[:ahead]]
        for idx, r0 in enumerate(starts):
            rows = slice(r0, r0 + rb)
            wd = width(r0)
            if idx + ahead < len(starts):
                pending.append(logits(starts[idx + ahead]))
            s = pending.pop(0) - ck_ref[:, 0:wd]
            if masked:
                qpos = r0 + lax.broadcasted_iota(jnp.int32, (rb, wd), 0)
                kpos = lax.broadcasted_iota(jnp.int32, (rb, wd), 1)
                s = jnp.where(kpos <= qpos, s, NEG_INF)
            m_prev = m_ref[rows, :]
            m_new = jnp.maximum(m_prev, jnp.max(s, axis=1, keepdims=True))
            alpha = jnp.exp2(m_prev - m_new)
            pr = jnp.exp2(s - jnp.tile(m_new, (1, wd // LANES)))
            l_ref[rows, :] = alpha * l_ref[rows, :] + jnp.sum(pr, axis=1, keepdims=True)
            acc_ref[rows, :] = alpha * acc_ref[rows, :] + jnp.dot(pr.astype(BF16), v_ref[0:wd, :],
                                                                  preferred_element_type=F32)
            m_ref[rows, :] = m_new

    @pl.when(ik < iq)
    def _():
        run(False)

    @pl.when(ik == iq)
    def _():
        run(True)
        o_ref[...] = (acc_ref[...] / l_ref[...]).astype(o_ref.dtype)


def fox_attention(q, kt, v, ck_t, bsz, seq, heads):
    t, d = q.shape
    hd = FOX_HEAD_DIM
    assert hd == LANES
    tb = _pick(seq, 2048)
    nb = seq // tb
    pairs = [(iq, ik) for iq in range(nb) for ik in range(iq + 1)]
    iq_tab = jnp.asarray([pr[0] for pr in pairs], jnp.int32)
    ik_tab = jnp.asarray([pr[1] for pr in pairs], jnp.int32)
    grid_spec = pltpu.PrefetchScalarGridSpec(
        num_scalar_prefetch=2,
        grid=(bsz, heads, len(pairs)),
        in_specs=[
            pl.BlockSpec((tb, hd), lambda b, h, s, iqt, ikt: (b * nb + iqt[s], h)),
            pl.BlockSpec((hd, tb), lambda b, h, s, iqt, ikt: (h, b * nb + ikt[s])),
            pl.BlockSpec((tb, hd), lambda b, h, s, iqt, ikt: (b * nb + ikt[s], h)),
            pl.BlockSpec((None, None, 1, tb), lambda b, h, s, iqt, ikt: (b, h, 0, ikt[s])),
        ],
        out_specs=pl.BlockSpec((tb, hd), lambda b, h, s, iqt, ikt: (b * nb + iqt[s], h)),
        scratch_shapes=[pltpu.VMEM((tb, LANES), F32), pltpu.VMEM((tb, LANES), F32), pltpu.VMEM((tb, hd), F32)],
    )
    return pl.pallas_call(
        functools.partial(_fox_attn_body, tb=tb, rb=_pick(tb, 128)),
        out_shape=jax.ShapeDtypeStruct((t, d), BF16),
        grid_spec=grid_spec,
        compiler_params=_params("parallel", "parallel", "arbitrary"),
        name="fox_attention",
    )(iq_tab, ik_tab, q, kt, v, ck_t)


def _top_rows(blocks, k, with_rank=False):
    out = []
    ranks = [jnp.full(w.shape, float(k), F32) for w in blocks] if with_rank else None
    for r in range(k):
        m = functools.reduce(jnp.maximum, blocks)
        m = jnp.max(m, axis=0, keepdims=True)
        out.append(m)
        hit = [w == m for w in blocks]
        if with_rank:
            ranks = [jnp.where(e, float(r), rk) for e, rk in zip(hit, ranks)]
        blocks = [jnp.where(e, NEG_INF, w) for e, w in zip(hit, blocks)]
    return (out, ranks) if with_rank else out


def _bf16_pair_words(x):
    u = pltpu.bitcast(x, jnp.uint32)
    lsb = lax.shift_right_logical(u, jnp.uint32(16)) & jnp.uint32(1)
    r = lax.shift_right_logical(u + jnp.uint32(0x7FFF) + lsb, jnp.uint32(16))
    return r | lax.shift_left(r, jnp.uint32(16))


def _peer_route_body(qt_ref, keys_ref, rows_ref, r2_ref, e2_ref, *, nkeys, kd):
    k = PEER_TOPK
    ni = PEER_KEYS_PER_STEP
    rb = SUBLANES
    for h in range(PEER_HEADS):
        base = h * 2 * kd
        s1 = jnp.dot(keys_ref[0], qt_ref[base:base + kd, :], preferred_element_type=F32)
        s2 = jnp.dot(keys_ref[1], qt_ref[base + kd:base + 2 * kd, :], preferred_element_type=F32)
        v1 = jnp.concatenate(_top_rows([s1[r:r + rb] for r in range(0, nkeys, rb)], k), axis=0)
        v2_rows, rank2 = _top_rows([s2[r:r + rb] for r in range(0, nkeys, rb)], k, with_rank=True)
        v2 = jnp.concatenate(v2_rows, axis=0)
        half = k // 2
        cands = [v1[0:1] + v2[0:half], v1[0:1] + v2[half:k]]
        cands += [v1[a:a + 1] + v2[0:half] for a in range(1, half)]
        cands += [v1[half:k] + v2[0:1]]
        sc = _top_rows(cands, k)
        m, tau = sc[0], sc[k - 1]
        z = jnp.zeros_like(m)
        for r in sc:
            z = z + jnp.exp(r - m)
        e1 = jnp.exp(s1 - v1[0:1]) / z
        cnt = jnp.zeros_like(s1)
        for b in range(k):
            cnt = cnt + jnp.where(s1 + v2_rows[b] >= tau, 1.0, 0.0)
        cnt_w, e1_w = _bf16_pair_words(cnt), _bf16_pair_words(e1)
        for blk in range(nkeys // ni):
            rows_ref[h, blk, 0:ni, :] = cnt_w[blk * ni:(blk + 1) * ni]
            rows_ref[h, blk, ni:2 * ni, :] = e1_w[blk * ni:(blk + 1) * ni]
        half_keys = nkeys // 2
        r2 = jnp.concatenate(rank2, axis=0).astype(BF16)
        e2 = jnp.exp(s2 - v2[0:1]).astype(BF16)
        r2_ref[h * half_keys:(h + 1) * half_keys, :] = pltpu.bitcast(r2, jnp.uint32)
        e2_ref[h * half_keys:(h + 1) * half_keys, :] = pltpu.bitcast(e2, jnp.uint32)


def peer_route(qt, keys):
    rows, t = qt.shape
    _, nkeys, kd = keys.shape
    ni = PEER_KEYS_PER_STEP
    tm = _pick(t, 256)
    big = jax.ShapeDtypeStruct((PEER_HEADS * nkeys // 2, t), jnp.uint32)
    big_spec = pl.BlockSpec((PEER_HEADS * nkeys // 2, tm), lambda i: (0, i))
    return pl.pallas_call(
        functools.partial(_peer_route_body, nkeys=nkeys, kd=kd),
        out_shape=(jax.ShapeDtypeStruct((PEER_HEADS, nkeys // ni, 2 * ni, t), jnp.uint32), big, big),
        grid=(t // tm,),
        in_specs=[pl.BlockSpec((rows, tm), lambda i: (0, i)),
                  pl.BlockSpec((2, nkeys, kd), lambda i: (0, 0, 0))],
        out_specs=(pl.BlockSpec((PEER_HEADS, nkeys // ni, 2 * ni, tm), lambda i: (0, 0, 0, i)),
                   big_spec, big_spec),
        compiler_params=_params("parallel"),
        name="peer_route",
    )(qt, keys)


def _gelu(x):
    return 0.5 * x * (1.0 + lax.erf(x * (1.0 / math.sqrt(2.0))))


def _peer_expert_body(xt_ref, u_ref, vt_ref, rows_ref, r2_ref, e2_ref, o_ref, act_ref, w_ref, *, nkeys, tm):
    e = pl.program_id(1)
    ni = PEER_KEYS_PER_STEP

    @pl.when(e == 0)
    def _():
        o_ref[...] = jnp.zeros_like(o_ref)
        act_ref[...] = jnp.zeros_like(act_ref)

    pack = 2 * SUBLANES
    half_keys = nkeys // 2

    def packed_rows(word_row):
        tile = pltpu.bitcast(jnp.broadcast_to(word_row, (SUBLANES, LANES)), BF16)
        return jnp.tile(tile, (nkeys // pack, 1))

    def gate_block(ii, sub):
        rows = slice(ii * nkeys, (ii + 1) * nkeys)
        cols = slice(sub * LANES, (sub + 1) * LANES)
        gate = jnp.zeros((nkeys, LANES), BF16)
        for h in range(PEER_HEADS):
            cnt = packed_rows(rows_ref[h, ii:ii + 1, cols])
            e1 = packed_rows(rows_ref[h, ni + ii:ni + ii + 1, cols])
            keys = slice(h * half_keys, (h + 1) * half_keys)
            r2 = pltpu.bitcast(r2_ref[keys, cols], BF16)
            e2 = pltpu.bitcast(e2_ref[keys, cols], BF16)
            gate = gate + jnp.where(r2 < cnt, e2, jnp.zeros((), BF16)) * e1
        w_ref[rows, cols] = (gate.astype(F32) * _gelu(act_ref[rows, cols])).astype(BF16)

    for ii in range(ni):
        for sub in range(tm // LANES):
            gate_block(ii, sub)

    act_ref[...] = jnp.dot(u_ref[...], xt_ref[...], preferred_element_type=F32)
    o_ref[...] += jnp.dot(vt_ref[...], w_ref[...], preferred_element_type=F32)


def peer_experts(xt, u, vt_blocks, rows, r2, e2):
    d, t = xt.shape
    n_exp = u.shape[0]
    nkeys = 2 * r2.shape[0] // PEER_HEADS
    ni = PEER_KEYS_PER_STEP
    tm = _pick(t, 512)
    n_e, _, eb = vt_blocks.shape
    assert eb == ni * nkeys and n_e * eb == n_exp
    once = dict(pipeline_mode=pl.Buffered(1))
    cur = lambda e: jnp.minimum(e, n_e - 1)
    prev = lambda e: jnp.maximum(e - 1, 0)
    route_spec = lambda: pl.BlockSpec((PEER_HEADS * nkeys // 2, tm), lambda i, e: (0, i), **once)
    return pl.pallas_call(
        functools.partial(_peer_expert_body, nkeys=nkeys, tm=tm),
        out_shape=jax.ShapeDtypeStruct((d, t), F32),
        grid=(t // tm, n_e + 1),
        in_specs=[pl.BlockSpec((d, tm), lambda i, e: (0, i), **once),
                  pl.BlockSpec((eb, d), lambda i, e: (cur(e), 0)),
                  pl.BlockSpec((None, d, eb), lambda i, e: (prev(e), 0, 0)),
                  pl.BlockSpec((PEER_HEADS, None, 2 * ni, tm), lambda i, e: (0, prev(e), 0, i)),
                  route_spec(), route_spec()],
        out_specs=pl.BlockSpec((d, tm), lambda i, e: (0, i)),
        scratch_shapes=[pltpu.VMEM((eb, tm), F32), pltpu.VMEM((eb, tm), BF16)],
        compiler_params=_params("parallel", "arbitrary"),
        name="peer_experts",
    )(xt, u, vt_blocks, rows, r2, e2)


def _peer_ffn_t(xt, w_q, subkeys, u, v):
    qt = matmul(w_q.T.astype(BF16), xt, BF16, name="peer_query")
    rows, r2, e2 = peer_route(qt, subkeys.astype(BF16))
    n_exp, d = v.shape
    eb = PEER_KEYS_PER_STEP * subkeys.shape[1]
    vt_blocks = v.astype(BF16).reshape(n_exp // eb, eb, d).transpose(0, 2, 1)
    return peer_experts(xt, u.astype(BF16), vt_blocks, rows, r2, e2)


def kernel(x, p, ln_mix, ssd_w_in, ssd_conv_w, ssd_conv_b, ssd_dt_bias, ssd_a_log, ssd_d_skip, ssd_norm_w, ssd_w_out, fox_w_in, fox_b_f, fox_w_out, ln_ffn, peer_w_q, peer_subkeys, peer_u, peer_v, ln_ple, ple_w_gate, ple_w_proj, final_norm):
    bsz, seq, d = x.shape
    depth = p.shape[0]
    t = bsz * seq
    h = x.reshape(t, d)
    for i in range(depth):
        j = i // 2
        if i % 2 == 0:
            (hn,) = rmsnorm(h, ln_mix[i])
            d_inner = ssd_w_out.shape[1]
            conv_dim = ssd_conv_w.shape[2]
            w_in = ssd_w_in[j]
            zxbc = matmul(hn, w_in[:, :d_inner + conv_dim].astype(BF16), BF16, name="ssd_in_proj")
            dt_raw = matmul(hn, w_in[:, d_inner + conv_dim:].astype(BF16), F32, name="ssd_dt_proj")
            y = ssd_core(zxbc, dt_raw, ssd_conv_w[j], ssd_conv_b[j], ssd_dt_bias[j], ssd_a_log[j],
                         ssd_d_skip[j], ssd_norm_w[j], bsz, seq)
            h = matmul(y, ssd_w_out[j].astype(BF16), F32, epilogue=_epi_residual, extra=(h,),
                       name="ssd_out_proj")
        else:
            hn, hnt = rmsnorm(h, ln_mix[i], emit_t=True)
            heads = fox_b_f.shape[1]
            w_in = fox_w_in[j]
            qscale = FOX_HEAD_DIM ** -0.5 * LOG2E
            q = matmul(hn, (w_in[:, :d] * qscale).astype(BF16), BF16, name="fox_q_proj")
            kt = matmul(w_in[:, d:2 * d].T.astype(BF16), hnt, BF16, name="fox_k_proj")
            v = matmul(hn, w_in[:, 2 * d:3 * d].astype(BF16), BF16, name="fox_v_proj")
            w_f = jnp.pad(w_in[:, 3 * d:], ((0, 0), (0, LANES - heads))).astype(BF16)
            f_raw = matmul(hn, w_f, F32, name="fox_gate_proj")
            b_f = jnp.pad(fox_b_f[j].astype(F32), (0, LANES - heads)).reshape(1, LANES)
            cum = fox_cum(f_raw, b_f, bsz, seq)
            ck_t = cum.reshape(bsz, seq, LANES)[:, :, :heads].transpose(0, 2, 1).reshape(bsz, heads, 1, seq)
            o = fox_attention(q, kt, v, ck_t, bsz, seq, heads)
            h = matmul(o, fox_w_out[j].astype(BF16), F32, epilogue=_epi_residual, extra=(h,),
                       name="fox_out_proj")
        (hnt,) = rmsnorm(h, ln_ffn[i], emit_n=False, emit_t=True)
        yt = _peer_ffn_t(hnt, peer_w_q[i], peer_subkeys[i], peer_u[i], peer_v[i])
        h, hn = rmsnorm(h, ln_ple[i], add=yt, add_transposed=True)
        p_i = p[i].reshape(t, -1).astype(BF16)
        w_proj = ple_w_proj[i].astype(BF16)
        pdim = p_i.shape[1]
        h = matmul(hn, ple_w_gate[i].astype(BF16), F32, epilogue=_epi_ple, extra=(h, p_i, w_proj),
                   extra_specs=lambda tm, tn: [pl.BlockSpec((tm, tn), lambda a, b, c: (a, b)),
                                               pl.BlockSpec((tm, pdim), lambda a, b, c: (a, 0)),
                                               pl.BlockSpec((pdim, tn), lambda a, b, c: (0, b))],
                   name="ple")
    (out,) = rmsnorm(h, final_norm, out_dtype=F32)
    return out.reshape(bsz, seq, d)
```
